```python
import math
import jax, jax.numpy as jnp
from jax import lax
import numpy as np

D_MODEL = 4096
BATCH = 8
SEQ = 2048
DEPTH = 2

HEAD_DIM = 128
A_HEADS = (3 * D_MODEL) // (8 * HEAD_DIM)
A_KV_HEADS = A_HEADS // 3
A_GROUP = A_HEADS // A_KV_HEADS
A_WINDOW = 128
A_BLOCK = 128
POOL_GROUPS = 4
POOL_WIDTH = D_MODEL // 4
POOL_GROUP_WIDTH = POOL_WIDTH // POOL_GROUPS
POOL_WINDOWS = (2, 4, 8, 16)
C_HEADS = (3 * D_MODEL) // (8 * HEAD_DIM)
C_HALF_DIM = HEAD_DIM // 2
C_Q_BLOCK = 128
A_Q_W = A_HEADS * HEAD_DIM
A_KV_W = A_KV_HEADS * HEAD_DIM
C_W = C_HEADS * HEAD_DIM
MIX_WIDTH = A_Q_W + POOL_WIDTH + C_W
IN_SIZES = (A_Q_W, A_KV_W, A_KV_W, POOL_WIDTH, C_W, C_W, C_W)
IN_WIDTH = sum(IN_SIZES)
IN_SPLITS = tuple(int(s) for s in np.cumsum(IN_SIZES)[:-1])
BRANCH_SPLITS = (A_Q_W, A_Q_W + POOL_WIDTH)
N_BRANCHES = 3
ADA_CHUNKS = 6
FFN_HIDDEN = -(-8 * D_MODEL // (3 * 256)) * 256
NEG_INF = -1e30
NORM_EPS = 1e-6
SUBLN_EPS = 1e-5

kernel_name = "hybrid_parallel_gated_encoder"


def rms_norm(x, g, eps=NORM_EPS):
    xf = x.astype(jnp.float32)
    y = xf * lax.rsqrt(jnp.mean(xf * xf, axis=-1, keepdims=True) + eps)
    return (y * g.astype(jnp.float32)).astype(x.dtype)


def modulate(h, shift, scale):
    return h * (1 + scale[:, None, :]) + shift[:, None, :]


def alibi_slopes(n):
    return jnp.asarray(2.0 ** (-8.0 * np.arange(1, n + 1) / n), dtype=jnp.float32)


def windowed_gqa_sink(q, k, v, sink):
    B, T = q.shape[0], q.shape[1]
    nb = T // A_BLOCK
    qb = q.reshape(B, nb, A_BLOCK, A_KV_HEADS, A_GROUP, HEAD_DIM)
    pad = ((0, 0), (A_BLOCK, A_BLOCK), (0, 0), (0, 0))
    kp = jnp.pad(k, pad).reshape(B, nb + 2, A_BLOCK, A_KV_HEADS, HEAD_DIM)
    vp = jnp.pad(v, pad).reshape(B, nb + 2, A_BLOCK, A_KV_HEADS, HEAD_DIM)
    kb = jnp.concatenate([kp[:, :-2], kp[:, 1:-1], kp[:, 2:]], axis=2)
    vb = jnp.concatenate([vp[:, :-2], vp[:, 1:-1], vp[:, 2:]], axis=2)
    logits = jnp.einsum('bnqhgd,bnshd->bnhgqs', qb, kb,
                        preferred_element_type=jnp.float32) * (HEAD_DIM ** -0.5)
    qi = jnp.arange(A_BLOCK)
    kj = jnp.arange(3 * A_BLOCK)
    rel = qi[:, None] + A_BLOCK - kj[None, :]
    kpos = jnp.arange(nb)[:, None] * A_BLOCK - A_BLOCK + kj[None, :]
    mask = (jnp.abs(rel) <= A_WINDOW)[None] & ((kpos >= 0) & (kpos < T))[:, None, :]
    slopes = alibi_slopes(A_HEADS).reshape(A_KV_HEADS, A_GROUP)
    bias = -slopes[:, :, None, None] * jnp.abs(rel).astype(jnp.float32)
    logits = jnp.where(mask[None, :, None, None], logits + bias, NEG_INF)
    s = sink.astype(jnp.float32).reshape(A_KV_HEADS, A_GROUP)[None, None, :, :, None, None]
    m = jnp.maximum(jnp.max(logits, axis=-1, keepdims=True), s)
    p = jnp.exp(logits - m)
    probs = p / (jnp.sum(p, axis=-1, keepdims=True) + jnp.exp(s - m))
    o = jnp.einsum('bnhgqs,bnshd->bnqhgd', probs.astype(v.dtype), vb)
    return o.reshape(B, T, A_Q_W)


def multiscale_pool(u, pool_w, pool_scale):
    B, T = u.shape[0], u.shape[1]
    ug = u.reshape(B, T, POOL_GROUPS, POOL_GROUP_WIDTH).astype(jnp.float32)
    csum = jnp.pad(jnp.cumsum(ug, axis=1), ((0, 0), (1, 0), (0, 0), (0, 0)))
    pos = jnp.arange(T)
    means = []
    for g, w in enumerate(POOL_WINDOWS):
        r = w // 2
        lo = jnp.maximum(pos - r, 0)
        hi = jnp.minimum(pos + r + 1, T)
        sg = csum[:, :, g]
        cnt = (hi - lo).astype(jnp.float32)[None, :, None]
        means.append((sg[:, hi] - sg[:, lo]) / cnt)
    pooled = jnp.stack(means, axis=2)
    z = (pooled - ug).astype(u.dtype)
    z = jnp.einsum('btgc,gcd->btgd', z, pool_w)
    return z.reshape(B, T, POOL_WIDTH) * pool_scale


def diff_attention(q, k, v, lam, subln_g, lam_init):
    B, T = q.shape[0], q.shape[1]
    nb = T // C_Q_BLOCK
    qb = q.reshape(B, nb, C_Q_BLOCK, C_HEADS, 2, C_HALF_DIM).transpose(1, 0, 2, 3, 4, 5)
    slopes = alibi_slopes(C_HEADS)[:, None, None, None]
    kpos = jnp.arange(T)
    scale = C_HALF_DIM ** -0.5

    def block(args):
        qblk, n = args
        logits = jnp.einsum('bqhcd,bshcd->bhcqs', qblk, k,
                            preferred_element_type=jnp.float32) * scale
        qpos = n * C_Q_BLOCK + jnp.arange(C_Q_BLOCK)
        dist = jnp.abs(qpos[:, None] - kpos[None, :]).astype(jnp.float32)
        probs = jax.nn.softmax(logits - slopes * dist, axis=-1)
        wts = probs[:, :, 0] - lam * probs[:, :, 1]
        return jnp.einsum('bhqs,bshd->bqhd', wts.astype(v.dtype), v)

    o = lax.map(block, (qb, jnp.arange(nb)))
    o = o.transpose(1, 0, 2, 3, 4).reshape(B, T, C_HEADS, HEAD_DIM)
    o = rms_norm(o, subln_g, eps=SUBLN_EPS) * (1 - lam_init)
    return o.reshape(B, T, C_W)


def hybrid_mixer(h, w_in, sink, pool_w, pool_scale, lam_vecs, subln_g,
                 w_gate, b_gate, w_branch, w_out, lam_init):
    B, T = h.shape[0], h.shape[1]
    proj = jnp.einsum('btd,de->bte', h, w_in)
    qa, ka, va, u, qc, kc, vc = jnp.split(proj, IN_SPLITS, axis=-1)
    oa = windowed_gqa_sink(qa.reshape(B, T, A_HEADS, HEAD_DIM),
                           ka.reshape(B, T, A_KV_HEADS, HEAD_DIM),
                           va.reshape(B, T, A_KV_HEADS, HEAD_DIM), sink)
    ob = multiscale_pool(u, pool_w, pool_scale)
    lv = lam_vecs.astype(jnp.float32)
    lam = jnp.exp(jnp.sum(lv[0] * lv[1])) - jnp.exp(jnp.sum(lv[2] * lv[3])) + lam_init
    oc = diff_attention(qc.reshape(B, T, C_HEADS, 2, C_HALF_DIM),
                        kc.reshape(B, T, C_HEADS, 2, C_HALF_DIM),
                        vc.reshape(B, T, C_HEADS, HEAD_DIM), lam, subln_g, lam_init)
    gates = jax.nn.sigmoid(jnp.einsum('btd,de->bte', h, w_gate) + b_gate)
    gates = gates.reshape(B, T, N_BRANCHES, D_MODEL)
    wa, wb, wc = jnp.split(w_branch, BRANCH_SPLITS, axis=0)
    merged = (gates[:, :, 0] * jnp.einsum('btk,kd->btd', oa, wa)
              + gates[:, :, 1] * jnp.einsum('btk,kd->btd', ob, wb)
              + gates[:, :, 2] * jnp.einsum('btk,kd->btd', oc, wc))
    return jnp.einsum('btd,de->bte', merged, w_out)


def swiglu(h, w_gate, w_up, w_down):
    a = jnp.einsum('btd,df->btf', h, w_gate)
    b = jnp.einsum('btd,df->btf', h, w_up)
    return jnp.einsum('btf,fd->btd', jax.nn.silu(a) * b, w_down)


def setup_inputs(seed: int = 0) -> dict:
    key = jax.random.key(seed)
    ks = jax.random.split(key, 21)
    D = D_MODEL

    def nrm(k, shape, scale):
        return jax.random.normal(k, shape, jnp.float32) * scale

    return {
        'x': nrm(ks[0], (BATCH, SEQ, D), 1.0),
        'c': nrm(ks[1], (BATCH, D), 1.0),
        'ada_w': nrm(ks[2], (DEPTH, D, ADA_CHUNKS * D), 0.5 * D ** -0.5),
        'ada_b': nrm(ks[3], (DEPTH, ADA_CHUNKS * D), 0.02),
        'mix_pre_g': 1.0 + nrm(ks[4], (DEPTH, D), 0.05),
        'mix_post_g': 1.0 + nrm(ks[5], (DEPTH, D), 0.05),
        'w_in': nrm(ks[6], (DEPTH, D, IN_WIDTH), D ** -0.5),
        'attn_sink': nrm(ks[7], (DEPTH, A_HEADS), 0.5),
        'pool_w': nrm(ks[8], (DEPTH, POOL_GROUPS, POOL_GROUP_WIDTH, POOL_GROUP_WIDTH), POOL_GROUP_WIDTH ** -0.5),
        'pool_scale': 1.0 + nrm(ks[9], (DEPTH, POOL_WIDTH), 0.1),
        'diff_lambda': nrm(ks[10], (DEPTH, 4, C_HALF_DIM), 0.1),
        'diff_subln_g': 1.0 + nrm(ks[11], (DEPTH, HEAD_DIM), 0.05),
        'w_gate': nrm(ks[12], (DEPTH, D, N_BRANCHES * D), D ** -0.5),
        'b_gate': nrm(ks[13], (DEPTH, N_BRANCHES * D), 0.02),
        'w_branch': nrm(ks[14], (DEPTH, MIX_WIDTH, D), C_W ** -0.5),
        'w_out': nrm(ks[15], (DEPTH, D, D), D ** -0.5),
        'ffn_pre_g': 1.0 + nrm(ks[16], (DEPTH, D), 0.05),
        'ffn_post_g': 1.0 + nrm(ks[17], (DEPTH, D), 0.05),
        'ffn_w_gate': nrm(ks[18], (DEPTH, D, FFN_HIDDEN), D ** -0.5),
        'ffn_w_up': nrm(ks[19], (DEPTH, D, FFN_HIDDEN), D ** -0.5),
        'ffn_w_down': nrm(ks[20], (DEPTH, FFN_HIDDEN, D), FFN_HIDDEN ** -0.5),
    }


def reference(x, c, ada_w, ada_b, mix_pre_g, mix_post_g, w_in, attn_sink, pool_w,
              pool_scale, diff_lambda, diff_subln_g, w_gate, b_gate, w_branch, w_out,
              ffn_pre_g, ffn_post_g, ffn_w_gate, ffn_w_up, ffn_w_down):
    cond = jax.nn.silu(c)
    for l in range(DEPTH):
        lam_init = 0.8 - 0.6 * math.exp(-0.3 * l)
        mod = jnp.einsum('bd,de->be', cond, ada_w[l]) + ada_b[l]
        shift1, scale1, gate1, shift2, scale2, gate2 = jnp.split(mod, ADA_CHUNKS, axis=-1)
        h = modulate(rms_norm(x, mix_pre_g[l]), shift1, scale1)
        y = hybrid_mixer(h, w_in[l], attn_sink[l], pool_w[l], pool_scale[l], diff_lambda[l],
                         diff_subln_g[l], w_gate[l], b_gate[l], w_branch[l], w_out[l], lam_init)
        x = x + gate1[:, None, :] * rms_norm(y, mix_post_g[l])
        h = modulate(rms_norm(x, ffn_pre_g[l]), shift2, scale2)
        y = swiglu(h, ffn_w_gate[l], ffn_w_up[l], ffn_w_down[l])
        x = x + gate2[:, None, :] * rms_norm(y, ffn_post_g[l])
    return x
```

```python
import functools
import math

import jax
import jax.numpy as jnp
import numpy as np
from jax import lax
from jax.experimental import pallas as pl
from jax.experimental.pallas import tpu as pltpu

D_MODEL = 4096
BATCH = 8
SEQ = 2048
DEPTH = 2
ROWS = BATCH * SEQ

HEAD_DIM = 128
A_HEADS = 12
A_KV_HEADS = 4
A_GROUP = 3
A_WINDOW = 128
A_BLOCK = 128
A_Q_W = A_HEADS * HEAD_DIM
A_KV_W = A_KV_HEADS * HEAD_DIM
POOL_GROUPS = 4
POOL_WIDTH = 1024
POOL_GROUP_WIDTH = 256
POOL_WINDOWS = (2, 4, 8, 16)
C_HEADS = 12
C_HALF_DIM = 64
C_W = C_HEADS * HEAD_DIM
IN_WIDTH = 8192
N_BRANCHES = 3
ADA_CHUNKS = 6
FFN_HIDDEN = 11008
NEG_INF = -1e30
NORM_EPS = 1e-6
SUBLN_EPS = 1e-5

V7X_VMEM_LIMIT_BYTES = 60 * 1024 * 1024

BF16 = jnp.bfloat16
F32 = jnp.float32


def _params(n_grid_dims):
    return pltpu.CompilerParams(
        dimension_semantics=("arbitrary",) * n_grid_dims,
        vmem_limit_bytes=V7X_VMEM_LIMIT_BYTES,
    )


def _alibi_slopes(n):
    return [float(2.0 ** (-8.0 * h / n)) for h in range(1, n + 1)]


ADA_BN = 512
ADA_ROWS = 16


def _ada_kernel(c_ref, w_ref, b_ref, o_ref):
    c = c_ref[...]
    cond = (c * jax.nn.sigmoid(c)).astype(BF16)
    w = w_ref[...].astype(BF16)
    o_ref[...] = jnp.dot(cond, w, preferred_element_type=F32) + b_ref[...]


def _ada_mod(c_pad, ada_w, ada_b):
    n = ADA_CHUNKS * D_MODEL
    return pl.pallas_call(
        _ada_kernel,
        grid=(DEPTH, n // ADA_BN),
        in_specs=[
            pl.BlockSpec((ADA_ROWS, D_MODEL), lambda l, j: (0, 0)),
            pl.BlockSpec((None, D_MODEL, ADA_BN), lambda l, j: (l, 0, j)),
            pl.BlockSpec((None, 1, ADA_BN), lambda l, j: (l, 0, j)),
        ],
        out_specs=pl.BlockSpec((None, ADA_ROWS, ADA_BN), lambda l, j: (l, 0, j)),
        out_shape=jax.ShapeDtypeStruct((DEPTH, ADA_ROWS, n), F32),
        compiler_params=_params(2),
        name="ada_mod",
    )(c_pad, ada_w, ada_b.reshape(DEPTH, 1, n))


NORM_ROWS = 256


def _rms(x, g, eps):
    return x * lax.rsqrt(jnp.mean(x * x, axis=-1, keepdims=True) + eps) * g


def _mod_index(layer, chunk):
    def index(i):
        b = (i * NORM_ROWS) // SEQ
        return ((layer * BATCH + b) * ADA_CHUNKS + chunk, 0, 0)
    return index


def _norm_mod_kernel(x_ref, g_ref, shift_ref, scale_ref, h_ref):
    h = _rms(x_ref[...], g_ref[...], NORM_EPS)
    h_ref[...] = (h * (1.0 + scale_ref[...]) + shift_ref[...]).astype(BF16)


def _norm_mod(x, g, mod, layer, shift_chunk):
    row_spec = pl.BlockSpec((NORM_ROWS, D_MODEL), lambda i: (i, 0))
    vec_spec = pl.BlockSpec((1, D_MODEL), lambda i: (0, 0))
    return pl.pallas_call(
        _norm_mod_kernel,
        grid=(ROWS // NORM_ROWS,),
        in_specs=[
            row_spec, vec_spec,
            pl.BlockSpec((None, 1, D_MODEL), _mod_index(layer, shift_chunk)),
            pl.BlockSpec((None, 1, D_MODEL), _mod_index(layer, shift_chunk + 1)),
        ],
        out_specs=row_spec,
        out_shape=jax.ShapeDtypeStruct((ROWS, D_MODEL), BF16),
        compiler_params=_params(1),
        name="norm_mod",
    )(x, g.reshape(1, D_MODEL), mod, mod)


def _resid_kernel(x_ref, y_ref, gate_ref, post_g_ref, o_ref):
    o_ref[...] = x_ref[...] + gate_ref[...] * _rms(y_ref[...], post_g_ref[...], NORM_EPS)


def _resid_next_kernel(x_ref, y_ref, gate_ref, post_g_ref, pre_g_ref,
                       shift_ref, scale_ref, o_ref, h_ref):
    x_new = x_ref[...] + gate_ref[...] * _rms(y_ref[...], post_g_ref[...], NORM_EPS)
    o_ref[...] = x_new
    h = _rms(x_new, pre_g_ref[...], NORM_EPS)
    h_ref[...] = (h * (1.0 + scale_ref[...]) + shift_ref[...]).astype(BF16)


def _resid(x, y, post_g, mod, layer, gate_chunk, nxt=None):
    row_spec = pl.BlockSpec((NORM_ROWS, D_MODEL), lambda i: (i, 0))
    vec_spec = pl.BlockSpec((1, D_MODEL), lambda i: (0, 0))
    gate_spec = pl.BlockSpec((None, 1, D_MODEL), _mod_index(layer, gate_chunk))
    if nxt is None:
        return pl.pallas_call(
            _resid_kernel,
            grid=(ROWS // NORM_ROWS,),
            in_specs=[row_spec, row_spec, gate_spec, vec_spec],
            out_specs=row_spec,
            out_shape=jax.ShapeDtypeStruct((ROWS, D_MODEL), F32),
            compiler_params=_params(1),
            name="resid",
        )(x, y, mod, post_g.reshape(1, D_MODEL))
    pre_g, nlayer, shift_chunk = nxt
    return pl.pallas_call(
        _resid_next_kernel,
        grid=(ROWS // NORM_ROWS,),
        in_specs=[
            row_spec, row_spec, gate_spec, vec_spec, vec_spec,
            pl.BlockSpec((None, 1, D_MODEL), _mod_index(nlayer, shift_chunk)),
            pl.BlockSpec((None, 1, D_MODEL), _mod_index(nlayer, shift_chunk + 1)),
        ],
        out_specs=[row_spec, row_spec],
        out_shape=[jax.ShapeDtypeStruct((ROWS, D_MODEL), F32),
                   jax.ShapeDtypeStruct((ROWS, D_MODEL), BF16)],
        compiler_params=_params(1),
        name="resid_next",
    )(x, y, mod, post_g.reshape(1, D_MODEL), pre_g.reshape(1, D_MODEL), mod, mod)


def _matmul_kernel(a_ref, w_ref, o_ref):
    o_ref[...] = jnp.dot(a_ref[...], w_ref[...],
                         preferred_element_type=F32).astype(o_ref.dtype)


def _matmul(a, w, out_dtype, bm, bn, n_off=0, n_blocks=None, name="matmul"):
    m, k = a.shape
    if n_blocks is None:
        n_blocks = w.shape[1] // bn - n_off
    return pl.pallas_call(
        _matmul_kernel,
        grid=(m // bm, n_blocks),
        in_specs=[
            pl.BlockSpec((bm, k), lambda i, j: (i, 0)),
            pl.BlockSpec((k, bn), lambda i, j: (0, j + n_off)),
        ],
        out_specs=pl.BlockSpec((bm, bn), lambda i, j: (i, j)),
        out_shape=jax.ShapeDtypeStruct((m, n_blocks * bn), out_dtype),
        compiler_params=_params(2),
        name=name,
    )(a, w)


A_KEYS = 3 * A_BLOCK


def _attn_a_kernel(sink_ref, q_ref, k_ref, v_ref, o_ref):
    n = pl.program_id(1)
    start = jnp.clip(n * A_BLOCK - A_BLOCK, 0, SEQ - A_KEYS)
    start = pl.multiple_of(start, A_BLOCK)
    qpos = n * A_BLOCK + lax.broadcasted_iota(jnp.int32, (A_BLOCK, A_KEYS), 0)
    kpos = start + lax.broadcasted_iota(jnp.int32, (A_BLOCK, A_KEYS), 1)
    dist = jnp.abs(qpos - kpos)
    mask = dist <= A_WINDOW
    distf = dist.astype(F32)
    slopes = _alibi_slopes(A_HEADS)
    scale = HEAD_DIM ** -0.5
    for g in range(A_KV_HEADS):
        k_g = k_ref[pl.ds(start, A_KEYS), g * HEAD_DIM:(g + 1) * HEAD_DIM]
        v_g = v_ref[pl.ds(start, A_KEYS), g * HEAD_DIM:(g + 1) * HEAD_DIM]
        for j in range(A_GROUP):
            h = g * A_GROUP + j
            q_h = q_ref[:, h * HEAD_DIM:(h + 1) * HEAD_DIM]
            logits = lax.dot_general(q_h, k_g, (((1,), (1,)), ((), ())),
                                     preferred_element_type=F32) * scale
            logits = jnp.where(mask, logits - slopes[h] * distf, NEG_INF)
            s = sink_ref[h]
            m = jnp.maximum(jnp.max(logits, axis=-1, keepdims=True), s)
            p = jnp.exp(logits - m)
            denom = jnp.sum(p, axis=-1, keepdims=True) + jnp.exp(s - m)
            o = jnp.dot(p.astype(BF16), v_g, preferred_element_type=F32) / denom
            o_ref[:, h * HEAD_DIM:(h + 1) * HEAD_DIM] = o.astype(BF16)


def _attn_a(proj_a, sink):
    return pl.pallas_call(
        _attn_a_kernel,
        grid=(BATCH, SEQ // A_BLOCK),
        in_specs=[
            pl.BlockSpec(memory_space=pltpu.SMEM),
            pl.BlockSpec((None, A_BLOCK, A_Q_W), lambda b, n: (b, n, 0)),
            pl.BlockSpec((None, SEQ, A_KV_W), lambda b, n: (b, 0, A_Q_W // A_KV_W)),
            pl.BlockSpec((None, SEQ, A_KV_W), lambda b, n: (b, 0, A_Q_W // A_KV_W + 1)),
        ],
        out_specs=pl.BlockSpec((None, A_BLOCK, A_Q_W), lambda b, n: (b, n, 0)),
        out_shape=jax.ShapeDtypeStruct((BATCH, SEQ, A_Q_W), BF16),
        compiler_params=_params(2),
        name="attn_window",
    )(sink, proj_a, proj_a, proj_a)


def _pool_kernel(u_ref, w_ref, scale_ref, o_ref):
    t = lax.broadcasted_iota(jnp.int32, (SEQ, 1), 0)
    for g, win in enumerate(POOL_WINDOWS):
        r = win // 2
        cols = slice(g * POOL_GROUP_WIDTH, (g + 1) * POOL_GROUP_WIDTH)
        ug = u_ref[:, cols]
        acc = ug
        for d in range(1, r + 1):
            below = pltpu.roll(ug, d, 0)
            above = pltpu.roll(ug, SEQ - d, 0)
            acc = acc + jnp.where(t >= d, below, 0.0) + jnp.where(t < SEQ - d, above, 0.0)
        cnt = (jnp.minimum(t + r + 1, SEQ) - jnp.maximum(t - r, 0)).astype(F32)
        z = (acc / cnt - ug).astype(BF16)
        y = jnp.dot(z, w_ref[g], preferred_element_type=F32) * scale_ref[:, cols]
        o_ref[:, cols] = y.astype(BF16)


def _pool(u, pool_w, pool_scale):
    return pl.pallas_call(
        _pool_kernel,
        grid=(BATCH,),
        in_specs=[
            pl.BlockSpec((None, SEQ, POOL_WIDTH), lambda b: (b, 0, 0)),
            pl.BlockSpec((POOL_GROUPS, POOL_GROUP_WIDTH, POOL_GROUP_WIDTH), lambda b: (0, 0, 0)),
            pl.BlockSpec((1, POOL_WIDTH), lambda b: (0, 0)),
        ],
        out_specs=pl.BlockSpec((None, SEQ, POOL_WIDTH), lambda b: (b, 0, 0)),
        out_shape=jax.ShapeDtypeStruct((BATCH, SEQ, POOL_WIDTH), BF16),
        compiler_params=_params(1),
        name="pool",
    )(u, pool_w, pool_scale.reshape(1, POOL_WIDTH))


C_BQ = 256


def _attn_c_kernel(lam_init, slopes_ref, lam_ref, g_ref, q_ref, k_ref, v_ref, o_ref):
    h = pl.program_id(1)
    qi = pl.program_id(2)
    lv = lam_ref[...]
    lam = (jnp.exp(jnp.sum(lv[0:1] * lv[1:2], axis=-1, keepdims=True))
           - jnp.exp(jnp.sum(lv[2:3] * lv[3:4], axis=-1, keepdims=True)) + lam_init)
    q = q_ref[...] * (C_HALF_DIM ** -0.5)
    lane = lax.broadcasted_iota(jnp.int32, (C_BQ, HEAD_DIM), 1)
    zero = jnp.zeros_like(q)
    q1 = jnp.where(lane < C_HALF_DIM, q, zero)
    q2 = jnp.where(lane >= C_HALF_DIM, q, zero)
    k = k_ref[...]
    qpos = (qi * C_BQ + lax.broadcasted_iota(jnp.int32, (C_BQ, 1), 0)).astype(F32)
    kpos = lax.broadcasted_iota(jnp.int32, (1, SEQ), 1).astype(F32)
    bias = slopes_ref[h] * jnp.abs(qpos - kpos)
    nt = (((1,), (1,)), ((), ()))

    def softmax_parts(qh):
        logits = lax.dot_general(qh, k, nt, preferred_element_type=F32) - bias
        m = jnp.max(logits, axis=-1, keepdims=True)
        p = jnp.exp(logits - m)
        return p, jnp.sum(p, axis=-1, keepdims=True)

    p1, s1 = softmax_parts(q1)
    p2, s2 = softmax_parts(q2)
    w = p1 * (1.0 / s1) - p2 * (lam / s2)
    o = jnp.dot(w.astype(BF16), v_ref[...], preferred_element_type=F32)
    o = _rms(o, g_ref[...], SUBLN_EPS) * (1.0 - lam_init)
    o_ref[...] = o.astype(BF16)


def _attn_c(proj_c, lam_vecs, subln_g, lam_init):
    slopes = jnp.asarray(_alibi_slopes(C_HEADS), F32)
    return pl.pallas_call(
        functools.partial(_attn_c_kernel, lam_init),
        grid=(BATCH, C_HEADS, SEQ // C_BQ),
        in_specs=[
            pl.BlockSpec(memory_space=pltpu.SMEM),
            pl.BlockSpec((4, C_HALF_DIM), lambda b, h, i: (0, 0)),
            pl.BlockSpec((1, HEAD_DIM), lambda b, h, i: (0, 0)),
            pl.BlockSpec((None, C_BQ, HEAD_DIM), lambda b, h, i: (b, i, h)),
            pl.BlockSpec((None, SEQ, HEAD_DIM), lambda b, h, i: (b, 0, C_HEADS + h)),
            pl.BlockSpec((None, SEQ, HEAD_DIM), lambda b, h, i: (b, 0, 2 * C_HEADS + h)),
        ],
        out_specs=pl.BlockSpec((None, C_BQ, HEAD_DIM), lambda b, h, i: (b, i, h)),
        out_shape=jax.ShapeDtypeStruct((BATCH, SEQ, C_W), BF16),
        compiler_params=_params(3),
        name="attn_diff",
    )(slopes, lam_vecs, subln_g.reshape(1, HEAD_DIM), proj_c, proj_c, proj_c)


MERGE_BM = 1024
MERGE_BN = 256


def _merge_kernel(h_ref, oa_ref, ob_ref, oc_ref, wg0_ref, wg1_ref, wg2_ref,
                  b0_ref, b1_ref, b2_ref, wa_ref, wb_ref, wc_ref, o_ref):
    h = h_ref[...]

    def branch(wg_ref, b_ref, x_ref, w_ref):
        gate = jax.nn.sigmoid(jnp.dot(h, wg_ref[...], preferred_element_type=F32) + b_ref[...])
        return gate * jnp.dot(x_ref[...], w_ref[...], preferred_element_type=F32)

    merged = (branch(wg0_ref, b0_ref, oa_ref, wa_ref)
              + branch(wg1_ref, b1_ref, ob_ref, wb_ref)
              + branch(wg2_ref, b2_ref, oc_ref, wc_ref))
    o_ref[...] = merged.astype(BF16)


def _merge(h, oa, ob, oc, w_gate, b_gate, wa, wb, wc):
    bm, bn = MERGE_BM, MERGE_BN
    nb = D_MODEL // bn
    row = lambda width: pl.BlockSpec((bm, width), lambda i, j: (i, 0))
    gate_w = lambda br: pl.BlockSpec((D_MODEL, bn), lambda i, j: (0, j + br * nb))
    gate_b = lambda br: pl.BlockSpec((1, bn), lambda i, j: (0, j + br * nb))
    branch_w = lambda width: pl.BlockSpec((width, bn), lambda i, j: (0, j))
    b_gate = b_gate.reshape(1, N_BRANCHES * D_MODEL)
    return pl.pallas_call(
        _merge_kernel,
        grid=(ROWS // bm, nb),
        in_specs=[
            row(D_MODEL), row(A_Q_W), row(POOL_WIDTH), row(C_W),
            gate_w(0), gate_w(1), gate_w(2),
            gate_b(0), gate_b(1), gate_b(2),
            branch_w(A_Q_W), branch_w(POOL_WIDTH), branch_w(C_W),
        ],
        out_specs=pl.BlockSpec((bm, bn), lambda i, j: (i, j)),
        out_shape=jax.ShapeDtypeStruct((ROWS, D_MODEL), BF16),
        compiler_params=_params(2),
        name="gated_merge",
    )(h, oa, ob, oc, w_gate, w_gate, w_gate, b_gate, b_gate, b_gate, wa, wb, wc)


FFN_BM = 1024
FFN_BF = 256


def _ffn_up_kernel(h_ref, wg_ref, wu_ref, o_ref):
    h = h_ref[...]
    a = jnp.dot(h, wg_ref[...], preferred_element_type=F32)
    b = jnp.dot(h, wu_ref[...], preferred_element_type=F32)
    o_ref[...] = (a * jax.nn.sigmoid(a) * b).astype(BF16)


def _ffn_up(h, wg, wu):
    bm, bf = FFN_BM, FFN_BF
    w_spec = pl.BlockSpec((D_MODEL, bf), lambda i, j: (0, j))
    return pl.pallas_call(
        _ffn_up_kernel,
        grid=(ROWS // bm, FFN_HIDDEN // bf),
        in_specs=[pl.BlockSpec((bm, D_MODEL), lambda i, j: (i, 0)), w_spec, w_spec],
        out_specs=pl.BlockSpec((bm, bf), lambda i, j: (i, j)),
        out_shape=jax.ShapeDtypeStruct((ROWS, FFN_HIDDEN), BF16),
        compiler_params=_params(2),
        name="ffn_up",
    )(h, wg, wu)


PROJ_BM = 1024
PROJ_BN = 512


def kernel(x, c, ada_w, ada_b, mix_pre_g, mix_post_g, w_in, attn_sink, pool_w, pool_scale, diff_lambda, diff_subln_g, w_gate, b_gate, w_branch, w_out, ffn_pre_g, ffn_post_g, ffn_w_gate, ffn_w_up, ffn_w_down):
    c_pad = jnp.pad(c, ((0, ADA_ROWS - BATCH), (0, 0)))
    mod = _ada_mod(c_pad, ada_w, ada_b)[:, :BATCH]
    mod = mod.reshape(DEPTH * BATCH * ADA_CHUNKS, 1, D_MODEL)

    x = x.reshape(ROWS, D_MODEL)
    h = _norm_mod(x, mix_pre_g[0], mod, 0, 0)
    a_blocks = (A_Q_W + 2 * A_KV_W) // PROJ_BN
    u_blocks = POOL_WIDTH // PROJ_BN
    for l in range(DEPTH):
        lam_init = 0.8 - 0.6 * math.exp(-0.3 * l)
        w_in_l = w_in[l].astype(BF16)
        proj_a = _matmul(h, w_in_l, BF16, PROJ_BM, PROJ_BN, 0, a_blocks, name="proj_a")
        proj_u = _matmul(h, w_in_l, F32, PROJ_BM, PROJ_BN, a_blocks, u_blocks, name="proj_u")
        proj_c = _matmul(h, w_in_l, BF16, PROJ_BM, PROJ_BN, a_blocks + u_blocks, None, name="proj_c")

        oa = _attn_a(proj_a.reshape(BATCH, SEQ, -1), attn_sink[l])
        ob = _pool(proj_u.reshape(BATCH, SEQ, POOL_WIDTH), pool_w[l].astype(BF16), pool_scale[l])
        oc = _attn_c(proj_c.reshape(BATCH, SEQ, -1), diff_lambda[l], diff_subln_g[l], lam_init)

        wb_l = w_branch[l]
        merged = _merge(
            h, oa.reshape(ROWS, A_Q_W), ob.reshape(ROWS, POOL_WIDTH), oc.reshape(ROWS, C_W),
            w_gate[l].astype(BF16), b_gate[l],
            wb_l[:A_Q_W].astype(BF16), wb_l[A_Q_W:A_Q_W + POOL_WIDTH].astype(BF16),
            wb_l[A_Q_W + POOL_WIDTH:].astype(BF16))
        y = _matmul(merged, w_out[l].astype(BF16), F32, 1024, 512, name="w_out")
        x, h = _resid(x, y, mix_post_g[l], mod, l, 2, nxt=(ffn_pre_g[l], l, 3))

        hid = _ffn_up(h, ffn_w_gate[l].astype(BF16), ffn_w_up[l].astype(BF16))
        y = _matmul(hid, ffn_w_down[l].astype(BF16), F32, 512, 512, name="ffn_down")
        if l + 1 < DEPTH:
            x, h = _resid(x, y, ffn_post_g[l], mod, l, 5, nxt=(mix_pre_g[l + 1], l + 1, 0))
        else:
            x = _resid(x, y, ffn_post_g[l], mod, l, 5)
    return x.reshape(BATCH, SEQ, D_MODEL)
```

```python
import functools
import math

import jax
import jax.numpy as jnp
import numpy as np
from jax import lax
from jax.experimental import pallas as pl
from jax.experimental.pallas import tpu as pltpu

D_MODEL = 4096
BATCH = 8
SEQ = 2048
DEPTH = 2
ROWS = BATCH * SEQ

HEAD_DIM = 128
A_HEADS = 12
A_KV_HEADS = 4
A_GROUP = 3
A_WINDOW = 128
A_BLOCK = 128
A_Q_W = A_HEADS * HEAD_DIM
A_KV_W = A_KV_HEADS * HEAD_DIM
POOL_GROUPS = 4
POOL_WIDTH = 1024
POOL_GROUP_WIDTH = 256
POOL_WINDOWS = (2, 4, 8, 16)
C_HEADS = 12
C_HALF_DIM = 64
C_W = C_HEADS * HEAD_DIM
IN_WIDTH = 8192
N_BRANCHES = 3
ADA_CHUNKS = 6
FFN_HIDDEN = 11008
NEG_INF = -1e30
NORM_EPS = 1e-6
SUBLN_EPS = 1e-5

V7X_VMEM_LIMIT_BYTES = 60 * 1024 * 1024

BF16 = jnp.bfloat16
F32 = jnp.float32


def _params(n_grid_dims):
    return pltpu.CompilerParams(
        dimension_semantics=("arbitrary",) * n_grid_dims,
        vmem_limit_bytes=V7X_VMEM_LIMIT_BYTES,
    )


def _alibi_slopes(n):
    return [float(2.0 ** (-8.0 * h / n)) for h in range(1, n + 1)]


ADA_BN = 512
ADA_ROWS = 16


def _ada_kernel(c_ref, w_ref, b_ref, o_ref):
    c = c_ref[...]
    cond = (c * jax.nn.sigmoid(c)).astype(BF16)
    w = w_ref[...].astype(BF16)
    o_ref[...] = jnp.dot(cond, w, preferred_element_type=F32) + b_ref[...]


def _ada_mod(c_pad, ada_w, ada_b):
    n = ADA_CHUNKS * D_MODEL
    return pl.pallas_call(
        _ada_kernel,
        grid=(DEPTH, n // ADA_BN),
        in_specs=[
            pl.BlockSpec((ADA_ROWS, D_MODEL), lambda l, j: (0, 0)),
            pl.BlockSpec((None, D_MODEL, ADA_BN), lambda l, j: (l, 0, j)),
            pl.BlockSpec((None, 1, ADA_BN), lambda l, j: (l, 0, j)),
        ],
        out_specs=pl.BlockSpec((None, ADA_ROWS, ADA_BN), lambda l, j: (l, 0, j)),
        out_shape=jax.ShapeDtypeStruct((DEPTH, ADA_ROWS, n), F32),
        compiler_params=_params(2),
        name="ada_mod",
    )(c_pad, ada_w, ada_b.reshape(DEPTH, 1, n))


NORM_ROWS = 256


def _rms(x, g, eps):
    return x * lax.rsqrt(jnp.mean(x * x, axis=-1, keepdims=True) + eps) * g


def _mod_index(layer, chunk):
    def index(i):
        b = (i * NORM_ROWS) // SEQ
        return ((layer * BATCH + b) * ADA_CHUNKS + chunk, 0, 0)
    return index


def _norm_mod_kernel(x_ref, g_ref, shift_ref, scale_ref, h_ref):
    h = _rms(x_ref[...], g_ref[...], NORM_EPS)
    h_ref[...] = (h * (1.0 + scale_ref[...]) + shift_ref[...]).astype(BF16)


def _norm_mod(x, g, mod, layer, shift_chunk):
    row_spec = pl.BlockSpec((NORM_ROWS, D_MODEL), lambda i: (i, 0))
    vec_spec = pl.BlockSpec((1, D_MODEL), lambda i: (0, 0))
    return pl.pallas_call(
        _norm_mod_kernel,
        grid=(ROWS // NORM_ROWS,),
        in_specs=[
            row_spec, vec_spec,
            pl.BlockSpec((None, 1, D_MODEL), _mod_index(layer, shift_chunk)),
            pl.BlockSpec((None, 1, D_MODEL), _mod_index(layer, shift_chunk + 1)),
        ],
        out_specs=row_spec,
        out_shape=jax.ShapeDtypeStruct((ROWS, D_MODEL), BF16),
        compiler_params=_params(1),
        name="norm_mod",
    )(x, g.reshape(1, D_MODEL), mod, mod)


def _resid_kernel(x_ref, y_ref, gate_ref, post_g_ref, o_ref):
    o_ref[...] = x_ref[...] + gate_ref[...] * _rms(y_ref[...], post_g_ref[...], NORM_EPS)


def _resid_next_kernel(x_ref, y_ref, gate_ref, post_g_ref, pre_g_ref,
                       shift_ref, scale_ref, o_ref, h_ref):
    x_new = x_ref[...] + gate_ref[...] * _rms(y_ref[...], post_g_ref[...], NORM_EPS)
    o_ref[...] = x_new
    h = _rms(x_new, pre_g_ref[...], NORM_EPS)
    h_ref[...] = (h * (1.0 + scale_ref[...]) + shift_ref[...]).astype(BF16)


def _resid(x, y, post_g, mod, layer, gate_chunk, nxt=None):
    row_spec = pl.BlockSpec((NORM_ROWS, D_MODEL), lambda i: (i, 0))
    vec_spec = pl.BlockSpec((1, D_MODEL), lambda i: (0, 0))
    gate_spec = pl.BlockSpec((None, 1, D_MODEL), _mod_index(layer, gate_chunk))
    if nxt is None:
        return pl.pallas_call(
            _resid_kernel,
            grid=(ROWS // NORM_ROWS,),
            in_specs=[row_spec, row_spec, gate_spec, vec_spec],
            out_specs=row_spec,
            out_shape=jax.ShapeDtypeStruct((ROWS, D_MODEL), F32),
            compiler_params=_params(1),
            name="resid",
        )(x, y, mod, post_g.reshape(1, D_MODEL))
    pre_g, nlayer, shift_chunk = nxt
    return pl.pallas_call(
        _resid_next_kernel,
        grid=(ROWS // NORM_ROWS,),
        in_specs=[
            row_spec, row_spec, gate_spec, vec_spec, vec_spec,
            pl.BlockSpec((None, 1, D_MODEL), _mod_index(nlayer, shift_chunk)),
            pl.BlockSpec((None, 1, D_MODEL), _mod_index(nlayer, shift_chunk + 1)),
        ],
        out_specs=[row_spec, row_spec],
        out_shape=[jax.ShapeDtypeStruct((ROWS, D_MODEL), F32),
                   jax.ShapeDtypeStruct((ROWS, D_MODEL), BF16)],
        compiler_params=_params(1),
        name="resid_next",
    )(x, y, mod, post_g.reshape(1, D_MODEL), pre_g.reshape(1, D_MODEL), mod, mod)


def _matmul_kernel(a_ref, w_ref, o_ref):
    o_ref[...] = jnp.dot(a_ref[...], w_ref[...],
                         preferred_element_type=F32).astype(o_ref.dtype)


def _matmul(a, w, out_dtype, bm, bn, n_off=0, n_blocks=None, name="matmul"):
    m, k = a.shape
    if n_blocks is None:
        n_blocks = w.shape[1] // bn - n_off
    return pl.pallas_call(
        _matmul_kernel,
        grid=(m // bm, n_blocks),
        in_specs=[
            pl.BlockSpec((bm, k), lambda i, j: (i, 0)),
            pl.BlockSpec((k, bn), lambda i, j: (0, j + n_off)),
        ],
        out_specs=pl.BlockSpec((bm, bn), lambda i, j: (i, j)),
        out_shape=jax.ShapeDtypeStruct((m, n_blocks * bn), out_dtype),
        compiler_params=_params(2),
        name=name,
    )(a, w)


def _matmul_ws_kernel(scaled_blocks, col_scale, a_ref, w_ref, o_ref, wb_ref):
    @pl.when(pl.program_id(1) == 0)
    def _():
        wb_ref[...] = w_ref[...].astype(BF16)

    acc = jnp.dot(a_ref[...], wb_ref[...], preferred_element_type=F32)
    if scaled_blocks:
        acc = acc * jnp.where(pl.program_id(0) < scaled_blocks, col_scale, 1.0)
    o_ref[...] = acc.astype(o_ref.dtype)


def _matmul_ws(a, w, layer, out_dtype, bm, bn, n_off=0, n_blocks=None,
               scaled_blocks=0, col_scale=1.0, name="matmul_ws"):
    m, k = a.shape
    if n_blocks is None:
        n_blocks = w.shape[2] // bn - n_off
    return pl.pallas_call(
        functools.partial(_matmul_ws_kernel, scaled_blocks, col_scale),
        grid=(n_blocks, m // bm),
        in_specs=[
            pl.BlockSpec((bm, k), lambda j, i: (i, 0)),
            pl.BlockSpec((None, k, bn), lambda j, i: (layer, 0, j + n_off)),
        ],
        out_specs=pl.BlockSpec((bm, bn), lambda j, i: (i, j)),
        out_shape=jax.ShapeDtypeStruct((m, n_blocks * bn), out_dtype),
        scratch_shapes=[pltpu.VMEM((k, bn), BF16)],
        compiler_params=_params(2),
        name=name,
    )(a, w)


A_KEYS = 3 * A_BLOCK
A_NB = SEQ // A_BLOCK
A_GROUP_ROWS = A_GROUP * A_BLOCK
LOG2E = math.log2(math.e)
A_Q_SCALE = HEAD_DIM ** -0.5 * LOG2E


def _attn_a_bias_kernel(o_ref):
    p = pl.program_id(0)
    r = lax.broadcasted_iota(jnp.int32, (A_BLOCK, A_KEYS), 0)
    c = lax.broadcasted_iota(jnp.int32, (A_BLOCK, A_KEYS), 1)
    dist = jnp.abs(r + p * A_BLOCK - c)
    distf = dist.astype(F32)
    slopes = _alibi_slopes(A_HEADS)
    for g in range(A_KV_HEADS):
        for j in range(A_GROUP):
            bias = jnp.where(dist <= A_WINDOW, (-slopes[g * A_GROUP + j] * LOG2E) * distf, NEG_INF)
            o_ref[g, j * A_BLOCK:(j + 1) * A_BLOCK, :] = bias


def _attn_a_bias():
    return pl.pallas_call(
        _attn_a_bias_kernel,
        grid=(3,),
        out_specs=pl.BlockSpec((None, A_KV_HEADS, A_GROUP_ROWS, A_KEYS), lambda p: (p, 0, 0, 0)),
        out_shape=jax.ShapeDtypeStruct((3, A_KV_HEADS, A_GROUP_ROWS, A_KEYS), F32),
        compiler_params=_params(1),
        name="attn_window_bias",
    )()


def _attn_a_kernel(sink_ref, bias_ref, q_ref, k_ref, v_ref, o_ref):
    n = pl.program_id(1)
    start = jnp.clip(n * A_BLOCK - A_BLOCK, 0, SEQ - A_KEYS)
    start = pl.multiple_of(start, A_BLOCK)
    nt = (((1,), (1,)), ((), ()))
    for g in range(A_KV_HEADS):
        cols = slice(g * HEAD_DIM, (g + 1) * HEAD_DIM)
        k_g = k_ref[pl.ds(start, A_KEYS), cols]
        v_g = v_ref[pl.ds(start, A_KEYS), cols]
        heads = range(g * A_GROUP, (g + 1) * A_GROUP)
        q_g = jnp.concatenate([q_ref[:, h * HEAD_DIM:(h + 1) * HEAD_DIM] for h in heads], axis=0)
        sink = jnp.concatenate(
            [jnp.full((A_BLOCK, 1), sink_ref[h] * LOG2E, F32) for h in heads], axis=0)
        logits = lax.dot_general(q_g, k_g, nt, preferred_element_type=F32) + bias_ref[g]
        m = jnp.maximum(jnp.max(logits, axis=-1, keepdims=True), sink)
        p = jnp.exp2(logits - m)
        denom = jnp.sum(p, axis=-1, keepdims=True) + jnp.exp2(sink - m)
        o = jnp.dot(p.astype(BF16), v_g, preferred_element_type=F32) * (1.0 / denom)
        for j, h in enumerate(heads):
            o_ref[:, h * HEAD_DIM:(h + 1) * HEAD_DIM] = o[j * A_BLOCK:(j + 1) * A_BLOCK].astype(BF16)


def _attn_a(proj_a, sink, bias):
    pattern = lambda n: jnp.minimum(n, 1) + jnp.maximum(n - (A_NB - 2), 0)
    return pl.pallas_call(
        _attn_a_kernel,
        grid=(BATCH, A_NB),
        in_specs=[
            pl.BlockSpec(memory_space=pltpu.SMEM),
            pl.BlockSpec((None, A_KV_HEADS, A_GROUP_ROWS, A_KEYS), lambda b, n: (pattern(n), 0, 0, 0)),
            pl.BlockSpec((None, A_BLOCK, A_Q_W), lambda b, n: (b, n, 0)),
            pl.BlockSpec((None, SEQ, A_KV_W), lambda b, n: (b, 0, A_Q_W // A_KV_W)),
            pl.BlockSpec((None, SEQ, A_KV_W), lambda b, n: (b, 0, A_Q_W // A_KV_W + 1)),
        ],
        out_specs=pl.BlockSpec((None, A_BLOCK, A_Q_W), lambda b, n: (b, n, 0)),
        out_shape=jax.ShapeDtypeStruct((BATCH, SEQ, A_Q_W), BF16),
        compiler_params=_params(2),
        name="attn_window",
    )(sink, bias, proj_a, proj_a, proj_a)


def _pool_kernel(u_ref, w_ref, scale_ref, o_ref):
    t = lax.broadcasted_iota(jnp.int32, (SEQ, 1), 0)
    for g, win in enumerate(POOL_WINDOWS):
        r = win // 2
        cols = slice(g * POOL_GROUP_WIDTH, (g + 1) * POOL_GROUP_WIDTH)
        ug = u_ref[:, cols]
        acc = ug
        for d in range(1, r + 1):
            below = pltpu.roll(ug, d, 0)
            above = pltpu.roll(ug, SEQ - d, 0)
            acc = acc + jnp.where(t >= d, below, 0.0) + jnp.where(t < SEQ - d, above, 0.0)
        cnt = (jnp.minimum(t + r + 1, SEQ) - jnp.maximum(t - r, 0)).astype(F32)
        z = (acc / cnt - ug).astype(BF16)
        y = jnp.dot(z, w_ref[g], preferred_element_type=F32) * scale_ref[:, cols]
        o_ref[:, cols] = y.astype(BF16)


def _pool(u, pool_w, pool_scale):
    return pl.pallas_call(
        _pool_kernel,
        grid=(BATCH,),
        in_specs=[
            pl.BlockSpec((None, SEQ, POOL_WIDTH), lambda b: (b, 0, 0)),
            pl.BlockSpec((POOL_GROUPS, POOL_GROUP_WIDTH, POOL_GROUP_WIDTH), lambda b: (0, 0, 0)),
            pl.BlockSpec((1, POOL_WIDTH), lambda b: (0, 0)),
        ],
        out_specs=pl.BlockSpec((None, SEQ, POOL_WIDTH), lambda b: (b, 0, 0)),
        out_shape=jax.ShapeDtypeStruct((BATCH, SEQ, POOL_WIDTH), BF16),
        compiler_params=_params(1),
        name="pool",
    )(u, pool_w, pool_scale.reshape(1, POOL_WIDTH))


C_BQ = 256
C_NQ = SEQ // C_BQ
C_BIAS_BLOCKS = 2 * C_NQ - 1
C_Q_SCALE = C_HALF_DIM ** -0.5 * LOG2E


def _attn_c_bias_kernel(slopes_ref, o_ref):
    s = slopes_ref[pl.program_id(0)] * LOG2E
    r = lax.broadcasted_iota(jnp.int32, (C_BQ, C_BQ), 0)
    cc = lax.broadcasted_iota(jnp.int32, (C_BQ, C_BQ), 1)
    for cb in range(C_BIAS_BLOCKS):
        dist = jnp.abs(r - cc - (cb - (C_NQ - 1)) * C_BQ)
        o_ref[cb] = s * dist.astype(F32)


def _attn_c_bias():
    slopes = jnp.asarray(_alibi_slopes(C_HEADS), F32)
    return pl.pallas_call(
        _attn_c_bias_kernel,
        grid=(C_HEADS,),
        in_specs=[pl.BlockSpec(memory_space=pltpu.SMEM)],
        out_specs=pl.BlockSpec((None, C_BIAS_BLOCKS, C_BQ, C_BQ), lambda h: (h, 0, 0, 0)),
        out_shape=jax.ShapeDtypeStruct((C_HEADS, C_BIAS_BLOCKS, C_BQ, C_BQ), F32),
        compiler_params=_params(1),
        name="attn_diff_bias",
    )(slopes)


def _attn_c_kernel(lam_init, lam_ref, g_ref, bias_ref, q_ref, k_ref, v_ref, o_ref, kt_ref):
    qi = pl.program_id(2)

    @pl.when(qi == 0)
    def _():
        kt_ref[...] = k_ref[...].astype(F32).T.astype(BF16)

    lv = lam_ref[...]
    lam = (jnp.exp(jnp.sum(lv[0:1] * lv[1:2], axis=-1, keepdims=True))
           - jnp.exp(jnp.sum(lv[2:3] * lv[3:4], axis=-1, keepdims=True)) + lam_init)
    q = q_ref[...]
    lane = lax.broadcasted_iota(jnp.int32, (C_BQ, HEAD_DIM), 1)
    zero = jnp.zeros_like(q)
    q1 = jnp.where(lane < C_HALF_DIM, q, zero)
    q2 = jnp.where(lane >= C_HALF_DIM, q, zero)
    kt = kt_ref[...]
    bias = jnp.concatenate(
        [bias_ref[C_NQ - 1 - qi + kb] for kb in range(C_NQ)], axis=1)

    def softmax_parts(qh):
        logits = jnp.dot(qh, kt, preferred_element_type=F32) - bias
        m = jnp.max(logits, axis=-1, keepdims=True)
        p = jnp.exp2(logits - m)
        return p, jnp.sum(p, axis=-1, keepdims=True)

    p1, s1 = softmax_parts(q1)
    p2, s2 = softmax_parts(q2)
    w = p1 * (1.0 / s1) - p2 * (lam / s2)
    o = jnp.dot(w.astype(BF16), v_ref[...], preferred_element_type=F32)
    o = _rms(o, g_ref[...], SUBLN_EPS) * (1.0 - lam_init)
    o_ref[...] = o.astype(BF16)


def _attn_c(proj_c, bias, lam_vecs, subln_g, lam_init):
    return pl.pallas_call(
        functools.partial(_attn_c_kernel, lam_init),
        grid=(BATCH, C_HEADS, C_NQ),
        in_specs=[
            pl.BlockSpec((4, C_HALF_DIM), lambda b, h, i: (0, 0)),
            pl.BlockSpec((1, HEAD_DIM), lambda b, h, i: (0, 0)),
            pl.BlockSpec((None, C_BIAS_BLOCKS, C_BQ, C_BQ), lambda b, h, i: (h, 0, 0, 0)),
            pl.BlockSpec((None, C_BQ, HEAD_DIM), lambda b, h, i: (b, i, h)),
            pl.BlockSpec((None, SEQ, HEAD_DIM), lambda b, h, i: (b, 0, C_HEADS + h)),
            pl.BlockSpec((None, SEQ, HEAD_DIM), lambda b, h, i: (b, 0, 2 * C_HEADS + h)),
        ],
        out_specs=pl.BlockSpec((None, C_BQ, HEAD_DIM), lambda b, h, i: (b, i, h)),
        out_shape=jax.ShapeDtypeStruct((BATCH, SEQ, C_W), BF16),
        scratch_shapes=[pltpu.VMEM((HEAD_DIM, SEQ), BF16)],
        compiler_params=_params(3),
        name="attn_diff",
    )(lam_vecs, subln_g.reshape(1, HEAD_DIM), bias, proj_c, proj_c, proj_c)


MERGE_BM = 1024
MERGE_BN = 256


def _merge_kernel(h_ref, oa_ref, ob_ref, oc_ref, wg0_ref, wg1_ref, wg2_ref,
                  b0_ref, b1_ref, b2_ref, wa_ref, wb_ref, wc_ref, o_ref):
    h = h_ref[...]

    def branch(wg_ref, b_ref, x_ref, w_ref):
        gate = jax.nn.sigmoid(jnp.dot(h, wg_ref[...], preferred_element_type=F32) + b_ref[...])
        return gate * jnp.dot(x_ref[...], w_ref[...], preferred_element_type=F32)

    merged = (branch(wg0_ref, b0_ref, oa_ref, wa_ref)
              + branch(wg1_ref, b1_ref, ob_ref, wb_ref)
              + branch(wg2_ref, b2_ref, oc_ref, wc_ref))
    o_ref[...] = merged.astype(BF16)


def _merge(h, oa, ob, oc, w_gate, b_gate, wa, wb, wc):
    bm, bn = MERGE_BM, MERGE_BN
    nb = D_MODEL // bn
    row = lambda width: pl.BlockSpec((bm, width), lambda i, j: (i, 0))
    gate_w = lambda br: pl.BlockSpec((D_MODEL, bn), lambda i, j: (0, j + br * nb))
    gate_b = lambda br: pl.BlockSpec((1, bn), lambda i, j: (0, j + br * nb))
    branch_w = lambda width: pl.BlockSpec((width, bn), lambda i, j: (0, j))
    b_gate = b_gate.reshape(1, N_BRANCHES * D_MODEL)
    return pl.pallas_call(
        _merge_kernel,
        grid=(ROWS // bm, nb),
        in_specs=[
            row(D_MODEL), row(A_Q_W), row(POOL_WIDTH), row(C_W),
            gate_w(0), gate_w(1), gate_w(2),
            gate_b(0), gate_b(1), gate_b(2),
            branch_w(A_Q_W), branch_w(POOL_WIDTH), branch_w(C_W),
        ],
        out_specs=pl.BlockSpec((bm, bn), lambda i, j: (i, j)),
        out_shape=jax.ShapeDtypeStruct((ROWS, D_MODEL), BF16),
        compiler_params=_params(2),
        name="gated_merge",
    )(h, oa, ob, oc, w_gate, w_gate, w_gate, b_gate, b_gate, b_gate, wa, wb, wc)


FFN_BM = 2048
FFN_BF = 256


def _ffn_up_kernel(h_ref, wg_ref, wu_ref, o_ref, wgb_ref, wub_ref):
    @pl.when(pl.program_id(1) == 0)
    def _():
        wgb_ref[...] = wg_ref[...].astype(BF16)
        wub_ref[...] = wu_ref[...].astype(BF16)

    h = h_ref[...]
    a = jnp.dot(h, wgb_ref[...], preferred_element_type=F32)
    b = jnp.dot(h, wub_ref[...], preferred_element_type=F32)
    o_ref[...] = (a * jax.nn.sigmoid(a) * b).astype(BF16)


def _ffn_up(h, wg, wu, layer):
    bm, bf = FFN_BM, FFN_BF
    w_spec = pl.BlockSpec((None, D_MODEL, bf), lambda j, i: (layer, 0, j))
    return pl.pallas_call(
        _ffn_up_kernel,
        grid=(FFN_HIDDEN // bf, ROWS // bm),
        in_specs=[pl.BlockSpec((bm, D_MODEL), lambda j, i: (i, 0)), w_spec, w_spec],
        out_specs=pl.BlockSpec((bm, bf), lambda j, i: (i, j)),
        out_shape=jax.ShapeDtypeStruct((ROWS, FFN_HIDDEN), BF16),
        scratch_shapes=[pltpu.VMEM((D_MODEL, bf), BF16), pltpu.VMEM((D_MODEL, bf), BF16)],
        compiler_params=_params(2),
        name="ffn_up",
    )(h, wg, wu)


PROJ_BM = 1024
PROJ_BN = 512


def kernel(x, c, ada_w, ada_b, mix_pre_g, mix_post_g, w_in, attn_sink, pool_w, pool_scale, diff_lambda, diff_subln_g, w_gate, b_gate, w_branch, w_out, ffn_pre_g, ffn_post_g, ffn_w_gate, ffn_w_up, ffn_w_down):
    c_pad = jnp.pad(c, ((0, ADA_ROWS - BATCH), (0, 0)))
    mod = _ada_mod(c_pad, ada_w, ada_b)[:, :BATCH]
    mod = mod.reshape(DEPTH * BATCH * ADA_CHUNKS, 1, D_MODEL)

    x = x.reshape(ROWS, D_MODEL)
    h = _norm_mod(x, mix_pre_g[0], mod, 0, 0)
    a_blocks = (A_Q_W + 2 * A_KV_W) // PROJ_BN
    u_blocks = POOL_WIDTH // PROJ_BN
    q_blocks = A_Q_W // PROJ_BN
    bias_a = _attn_a_bias()
    bias_c = _attn_c_bias()
    for l in range(DEPTH):
        lam_init = 0.8 - 0.6 * math.exp(-0.3 * l)
        proj_a = _matmul_ws(h, w_in, l, BF16, PROJ_BM, PROJ_BN, 0, a_blocks,
                            scaled_blocks=q_blocks, col_scale=A_Q_SCALE, name="proj_a")
        proj_u = _matmul_ws(h, w_in, l, F32, PROJ_BM, PROJ_BN, a_blocks, u_blocks, name="proj_u")
        proj_c = _matmul_ws(h, w_in, l, BF16, PROJ_BM, PROJ_BN, a_blocks + u_blocks, None,
                            scaled_blocks=q_blocks, col_scale=C_Q_SCALE, name="proj_c")

        oa = _attn_a(proj_a.reshape(BATCH, SEQ, -1), attn_sink[l], bias_a)
        ob = _pool(proj_u.reshape(BATCH, SEQ, POOL_WIDTH), pool_w[l].astype(BF16), pool_scale[l])
        oc = _attn_c(proj_c.reshape(BATCH, SEQ, -1), bias_c, diff_lambda[l], diff_subln_g[l], lam_init)

        wb_l = w_branch[l]
        merged = _merge(
            h, oa.reshape(ROWS, A_Q_W), ob.reshape(ROWS, POOL_WIDTH), oc.reshape(ROWS, C_W),
            w_gate[l].astype(BF16), b_gate[l],
            wb_l[:A_Q_W].astype(BF16), wb_l[A_Q_W:A_Q_W + POOL_WIDTH].astype(BF16),
            wb_l[A_Q_W + POOL_WIDTH:].astype(BF16))
        y = _matmul_ws(merged, w_out, l, F32, 1024, 512, name="w_out")
        x, h = _resid(x, y, mix_post_g[l], mod, l, 2, nxt=(ffn_pre_g[l], l, 3))

        hid = _ffn_up(h, ffn_w_gate, ffn_w_up, l)
        y = _matmul(hid, ffn_w_down[l].astype(BF16), F32, 512, 512, name="ffn_down")
        if l + 1 < DEPTH:
            x, h = _resid(x, y, ffn_post_g[l], mod, l, 5, nxt=(mix_pre_g[l + 1], l + 1, 0))
        else:
            x = _resid(x, y, ffn_post_g[l], mod, l, 5)
    return x.reshape(BATCH, SEQ, D_MODEL)
```

```python
import functools
import math

import jax
import jax.numpy as jnp
import numpy as np
from jax import lax
from jax.experimental import pallas as pl
from jax.experimental.pallas import tpu as pltpu

D_MODEL = 4096
BATCH = 8
SEQ = 2048
DEPTH = 2
ROWS = BATCH * SEQ

HEAD_DIM = 128
A_HEADS = 12
A_KV_HEADS = 4
A_GROUP = 3
A_WINDOW = 128
A_BLOCK = 128
A_Q_W = A_HEADS * HEAD_DIM
A_KV_W = A_KV_HEADS * HEAD_DIM
POOL_GROUPS = 4
POOL_WIDTH = 1024
POOL_GROUP_WIDTH = 256
POOL_WINDOWS = (2, 4, 8, 16)
C_HEADS = 12
C_HALF_DIM = 64
C_W = C_HEADS * HEAD_DIM
IN_WIDTH = 8192
N_BRANCHES = 3
ADA_CHUNKS = 6
FFN_HIDDEN = 11008
NEG_INF = -1e30
NORM_EPS = 1e-6
SUBLN_EPS = 1e-5

V7X_VMEM_LIMIT_BYTES = 60 * 1024 * 1024

BF16 = jnp.bfloat16
F32 = jnp.float32


def _params(n_grid_dims):
    return pltpu.CompilerParams(
        dimension_semantics=("arbitrary",) * n_grid_dims,
        vmem_limit_bytes=V7X_VMEM_LIMIT_BYTES,
    )


def _alibi_slopes(n):
    return [float(2.0 ** (-8.0 * h / n)) for h in range(1, n + 1)]


ADA_BN = 512
ADA_ROWS = 16


def _ada_kernel(c_ref, w_ref, b_ref, o_ref):
    c = c_ref[...]
    cond = (c * jax.nn.sigmoid(c)).astype(BF16)
    w = w_ref[...].astype(BF16)
    o_ref[...] = jnp.dot(cond, w, preferred_element_type=F32) + b_ref[...]


def _ada_mod(c_pad, ada_w, ada_b):
    n = ADA_CHUNKS * D_MODEL
    return pl.pallas_call(
        _ada_kernel,
        grid=(DEPTH, n // ADA_BN),
        in_specs=[
            pl.BlockSpec((ADA_ROWS, D_MODEL), lambda l, j: (0, 0)),
            pl.BlockSpec((None, D_MODEL, ADA_BN), lambda l, j: (l, 0, j)),
            pl.BlockSpec((None, 1, ADA_BN), lambda l, j: (l, 0, j)),
        ],
        out_specs=pl.BlockSpec((None, ADA_ROWS, ADA_BN), lambda l, j: (l, 0, j)),
        out_shape=jax.ShapeDtypeStruct((DEPTH, ADA_ROWS, n), F32),
        compiler_params=_params(2),
        name="ada_mod",
    )(c_pad, ada_w, ada_b.reshape(DEPTH, 1, n))


NORM_ROWS = 256


def _rms(x, g, eps):
    return x * lax.rsqrt(jnp.mean(x * x, axis=-1, keepdims=True) + eps) * g


def _mod_index(layer, chunk):
    def index(i):
        b = (i * NORM_ROWS) // SEQ
        return ((layer * BATCH + b) * ADA_CHUNKS + chunk, 0, 0)
    return index


def _norm_mod_kernel(x_ref, g_ref, shift_ref, scale_ref, h_ref):
    h = _rms(x_ref[...], g_ref[...], NORM_EPS)
    h_ref[...] = (h * (1.0 + scale_ref[...]) + shift_ref[...]).astype(BF16)


def _norm_mod(x, g, mod, layer, shift_chunk):
    row_spec = pl.BlockSpec((NORM_ROWS, D_MODEL), lambda i: (i, 0))
    vec_spec = pl.BlockSpec((1, D_MODEL), lambda i: (0, 0))
    return pl.pallas_call(
        _norm_mod_kernel,
        grid=(ROWS // NORM_ROWS,),
        in_specs=[
            row_spec, vec_spec,
            pl.BlockSpec((None, 1, D_MODEL), _mod_index(layer, shift_chunk)),
            pl.BlockSpec((None, 1, D_MODEL), _mod_index(layer, shift_chunk + 1)),
        ],
        out_specs=row_spec,
        out_shape=jax.ShapeDtypeStruct((ROWS, D_MODEL), BF16),
        compiler_params=_params(1),
        name="norm_mod",
    )(x, g.reshape(1, D_MODEL), mod, mod)


def _resid_kernel(x_ref, y_ref, gate_ref, post_g_ref, o_ref):
    o_ref[...] = x_ref[...] + gate_ref[...] * _rms(y_ref[...], post_g_ref[...], NORM_EPS)


def _resid_next_kernel(x_ref, y_ref, gate_ref, post_g_ref, pre_g_ref,
                       shift_ref, scale_ref, o_ref, h_ref):
    x_new = x_ref[...] + gate_ref[...] * _rms(y_ref[...], post_g_ref[...], NORM_EPS)
    o_ref[...] = x_new
    h = _rms(x_new, pre_g_ref[...], NORM_EPS)
    h_ref[...] = (h * (1.0 + scale_ref[...]) + shift_ref[...]).astype(BF16)


def _resid(x, y, post_g, mod, layer, gate_chunk, nxt=None):
    row_spec = pl.BlockSpec((NORM_ROWS, D_MODEL), lambda i: (i, 0))
    vec_spec = pl.BlockSpec((1, D_MODEL), lambda i: (0, 0))
    gate_spec = pl.BlockSpec((None, 1, D_MODEL), _mod_index(layer, gate_chunk))
    if nxt is None:
        return pl.pallas_call(
            _resid_kernel,
            grid=(ROWS // NORM_ROWS,),
            in_specs=[row_spec, row_spec, gate_spec, vec_spec],
            out_specs=row_spec,
            out_shape=jax.ShapeDtypeStruct((ROWS, D_MODEL), F32),
            compiler_params=_params(1),
            name="resid",
        )(x, y, mod, post_g.reshape(1, D_MODEL))
    pre_g, nlayer, shift_chunk = nxt
    return pl.pallas_call(
        _resid_next_kernel,
        grid=(ROWS // NORM_ROWS,),
        in_specs=[
            row_spec, row_spec, gate_spec, vec_spec, vec_spec,
            pl.BlockSpec((None, 1, D_MODEL), _mod_index(nlayer, shift_chunk)),
            pl.BlockSpec((None, 1, D_MODEL), _mod_index(nlayer, shift_chunk + 1)),
        ],
        out_specs=[row_spec, row_spec],
        out_shape=[jax.ShapeDtypeStruct((ROWS, D_MODEL), F32),
                   jax.ShapeDtypeStruct((ROWS, D_MODEL), BF16)],
        compiler_params=_params(1),
        name="resid_next",
    )(x, y, mod, post_g.reshape(1, D_MODEL), pre_g.reshape(1, D_MODEL), mod, mod)


def _matmul_kernel(a_ref, w_ref, o_ref):
    o_ref[...] = jnp.dot(a_ref[...], w_ref[...],
                         preferred_element_type=F32).astype(o_ref.dtype)


def _matmul(a, w, out_dtype, bm, bn, n_off=0, n_blocks=None, name="matmul"):
    m, k = a.shape
    if n_blocks is None:
        n_blocks = w.shape[1] // bn - n_off
    return pl.pallas_call(
        _matmul_kernel,
        grid=(m // bm, n_blocks),
        in_specs=[
            pl.BlockSpec((bm, k), lambda i, j: (i, 0)),
            pl.BlockSpec((k, bn), lambda i, j: (0, j + n_off)),
        ],
        out_specs=pl.BlockSpec((bm, bn), lambda i, j: (i, j)),
        out_shape=jax.ShapeDtypeStruct((m, n_blocks * bn), out_dtype),
        compiler_params=_params(2),
        name=name,
    )(a, w)


def _matmul_ws_kernel(scaled_blocks, col_scale, a_ref, w_ref, o_ref, wb_ref):
    @pl.when(pl.program_id(1) == 0)
    def _():
        wb_ref[...] = w_ref[...].astype(BF16)

    acc = jnp.dot(a_ref[...], wb_ref[...], preferred_element_type=F32)
    if scaled_blocks:
        acc = acc * jnp.where(pl.program_id(0) < scaled_blocks, col_scale, 1.0)
    o_ref[...] = acc.astype(o_ref.dtype)


def _matmul_ws(a, w, layer, out_dtype, bm, bn, n_off=0, n_blocks=None,
               scaled_blocks=0, col_scale=1.0, name="matmul_ws"):
    m, k = a.shape
    if n_blocks is None:
        n_blocks = w.shape[2] // bn - n_off
    return pl.pallas_call(
        functools.partial(_matmul_ws_kernel, scaled_blocks, col_scale),
        grid=(n_blocks, m // bm),
        in_specs=[
            pl.BlockSpec((bm, k), lambda j, i: (i, 0)),
            pl.BlockSpec((None, k, bn), lambda j, i: (layer, 0, j + n_off)),
        ],
        out_specs=pl.BlockSpec((bm, bn), lambda j, i: (i, j)),
        out_shape=jax.ShapeDtypeStruct((m, n_blocks * bn), out_dtype),
        scratch_shapes=[pltpu.VMEM((k, bn), BF16)],
        compiler_params=_params(2),
        name=name,
    )(a, w)


PROJ_BM = 1024
PROJ_BN = 512
PROJ_TILES = IN_WIDTH // PROJ_BN
PROJ_STEPS = ROWS // PROJ_BM
PROJ_A_TILES = (A_Q_W + 2 * A_KV_W) // PROJ_BN
PROJ_U_TILES = POOL_WIDTH // PROJ_BN
PROJ_Q_TILES = A_Q_W // PROJ_BN
SIDE_ROWS = D_MODEL // PROJ_STEPS
GATE_SIDE_COLS = N_BRANCHES * D_MODEL // PROJ_TILES
BRANCH_SIDE_COLS = D_MODEL // PROJ_TILES
BRANCH_B_ROW = 2048
BRANCH_C_ROW = 3072
BRANCH_ROWS = BRANCH_C_ROW + C_W


def _proj_kernel(h_ref, w_ref, wg_ref, wbr_ref, p_ref, u_ref, wg_o, wbr_o, wb_scr):
    j = pl.program_id(0)

    @pl.when(pl.program_id(1) == 0)
    def _():
        wb_scr[...] = w_ref[...].astype(BF16)

    wg_o[...] = wg_ref[...].astype(BF16)
    wbr_o[...] = wbr_ref[...].astype(BF16)
    acc = jnp.dot(h_ref[...], wb_scr[...], preferred_element_type=F32)
    c0 = PROJ_A_TILES + PROJ_U_TILES
    scale = jnp.where(j < PROJ_Q_TILES, A_Q_SCALE,
                      jnp.where((j >= c0) & (j < c0 + PROJ_Q_TILES), C_Q_SCALE, 1.0))
    p_ref[...] = (acc * scale).astype(BF16)

    @pl.when((j >= PROJ_A_TILES) & (j < c0))
    def _():
        u_ref[...] = acc


def _proj(h, w_in, w_gate, w_branch, layer):
    bm, bn = PROJ_BM, PROJ_BN
    last = PROJ_STEPS - 1
    c0 = PROJ_A_TILES + PROJ_U_TILES
    u_index = lambda j, i: (jnp.where(j < PROJ_A_TILES, 0, jnp.where(j < c0, i, last)),
                            jnp.clip(j - PROJ_A_TILES, 0, PROJ_U_TILES - 1))
    branch_row = lambda i: i + (BRANCH_B_ROW - A_Q_W) // SIDE_ROWS * (i >= A_Q_W // SIDE_ROWS)
    return pl.pallas_call(
        _proj_kernel,
        grid=(PROJ_TILES, PROJ_STEPS),
        in_specs=[
            pl.BlockSpec((bm, D_MODEL), lambda j, i: (i, 0)),
            pl.BlockSpec((None, D_MODEL, bn), lambda j, i: (layer, 0, j)),
            pl.BlockSpec((None, SIDE_ROWS, GATE_SIDE_COLS), lambda j, i: (layer, i, j)),
            pl.BlockSpec((None, SIDE_ROWS, BRANCH_SIDE_COLS), lambda j, i: (layer, i, j)),
        ],
        out_specs=[
            pl.BlockSpec((bm, bn), lambda j, i: (i, j)),
            pl.BlockSpec((bm, bn), u_index),
            pl.BlockSpec((SIDE_ROWS, GATE_SIDE_COLS), lambda j, i: (i, j)),
            pl.BlockSpec((SIDE_ROWS, BRANCH_SIDE_COLS), lambda j, i: (branch_row(i), j)),
        ],
        out_shape=[
            jax.ShapeDtypeStruct((ROWS, IN_WIDTH), BF16),
            jax.ShapeDtypeStruct((ROWS, PROJ_U_TILES * bn), F32),
            jax.ShapeDtypeStruct((D_MODEL, N_BRANCHES * D_MODEL), BF16),
            jax.ShapeDtypeStruct((BRANCH_ROWS, D_MODEL), BF16),
        ],
        scratch_shapes=[pltpu.VMEM((D_MODEL, bn), BF16)],
        compiler_params=_params(2),
        name="proj",
    )(h, w_in, w_gate, w_branch)


A_KEYS = 3 * A_BLOCK
A_NB = SEQ // A_BLOCK
A_GROUP_ROWS = A_GROUP * A_BLOCK
LOG2E = math.log2(math.e)
A_Q_SCALE = HEAD_DIM ** -0.5 * LOG2E


def _attn_a_bias_kernel(o_ref):
    p = pl.program_id(0)
    r = lax.broadcasted_iota(jnp.int32, (A_BLOCK, A_KEYS), 0)
    c = lax.broadcasted_iota(jnp.int32, (A_BLOCK, A_KEYS), 1)
    dist = jnp.abs(r + p * A_BLOCK - c)
    distf = dist.astype(F32)
    slopes = _alibi_slopes(A_HEADS)
    for g in range(A_KV_HEADS):
        for j in range(A_GROUP):
            bias = jnp.where(dist <= A_WINDOW, (-slopes[g * A_GROUP + j] * LOG2E) * distf, NEG_INF)
            o_ref[g, j * A_BLOCK:(j + 1) * A_BLOCK, :] = bias


def _attn_a_bias():
    return pl.pallas_call(
        _attn_a_bias_kernel,
        grid=(3,),
        out_specs=pl.BlockSpec((None, A_KV_HEADS, A_GROUP_ROWS, A_KEYS), lambda p: (p, 0, 0, 0)),
        out_shape=jax.ShapeDtypeStruct((3, A_KV_HEADS, A_GROUP_ROWS, A_KEYS), F32),
        compiler_params=_params(1),
        name="attn_window_bias",
    )()


def _attn_a_kernel(sink_ref, bias_ref, q_ref, k_ref, v_ref, o_ref):
    n = pl.program_id(1)
    start = jnp.clip(n * A_BLOCK - A_BLOCK, 0, SEQ - A_KEYS)
    start = pl.multiple_of(start, A_BLOCK)
    nt = (((1,), (1,)), ((), ()))
    for g in range(A_KV_HEADS):
        cols = slice(g * HEAD_DIM, (g + 1) * HEAD_DIM)
        k_g = k_ref[pl.ds(start, A_KEYS), cols]
        v_g = v_ref[pl.ds(start, A_KEYS), cols]
        heads = range(g * A_GROUP, (g + 1) * A_GROUP)
        q_g = jnp.concatenate([q_ref[:, h * HEAD_DIM:(h + 1) * HEAD_DIM] for h in heads], axis=0)
        sink = jnp.concatenate(
            [jnp.full((A_BLOCK, 1), sink_ref[h] * LOG2E, F32) for h in heads], axis=0)
        logits = lax.dot_general(q_g, k_g, nt, preferred_element_type=F32) + bias_ref[g]
        m = jnp.maximum(jnp.max(logits, axis=-1, keepdims=True), sink)
        p = jnp.exp2(logits - m)
        denom = jnp.sum(p, axis=-1, keepdims=True) + jnp.exp2(sink - m)
        o = jnp.dot(p.astype(BF16), v_g, preferred_element_type=F32) * (1.0 / denom)
        for j, h in enumerate(heads):
            o_ref[:, h * HEAD_DIM:(h + 1) * HEAD_DIM] = o[j * A_BLOCK:(j + 1) * A_BLOCK].astype(BF16)


def _attn_a(proj_a, sink, bias):
    pattern = lambda n: jnp.minimum(n, 1) + jnp.maximum(n - (A_NB - 2), 0)
    return pl.pallas_call(
        _attn_a_kernel,
        grid=(BATCH, A_NB),
        in_specs=[
            pl.BlockSpec(memory_space=pltpu.SMEM),
            pl.BlockSpec((None, A_KV_HEADS, A_GROUP_ROWS, A_KEYS), lambda b, n: (pattern(n), 0, 0, 0)),
            pl.BlockSpec((None, A_BLOCK, A_Q_W), lambda b, n: (b, n, 0)),
            pl.BlockSpec((None, SEQ, A_KV_W), lambda b, n: (b, 0, A_Q_W // A_KV_W)),
            pl.BlockSpec((None, SEQ, A_KV_W), lambda b, n: (b, 0, A_Q_W // A_KV_W + 1)),
        ],
        out_specs=pl.BlockSpec((None, A_BLOCK, A_Q_W), lambda b, n: (b, n, 0)),
        out_shape=jax.ShapeDtypeStruct((BATCH, SEQ, A_Q_W), BF16),
        compiler_params=_params(2),
        name="attn_window",
    )(sink, bias, proj_a, proj_a, proj_a)


def _pool_kernel(u_ref, w_ref, scale_ref, o_ref):
    t = lax.broadcasted_iota(jnp.int32, (SEQ, 1), 0)
    for g, win in enumerate(POOL_WINDOWS):
        r = win // 2
        cols = slice(g * POOL_GROUP_WIDTH, (g + 1) * POOL_GROUP_WIDTH)
        ug = u_ref[:, cols]
        acc = ug
        for d in range(1, r + 1):
            below = pltpu.roll(ug, d, 0)
            above = pltpu.roll(ug, SEQ - d, 0)
            acc = acc + jnp.where(t >= d, below, 0.0) + jnp.where(t < SEQ - d, above, 0.0)
        cnt = (jnp.minimum(t + r + 1, SEQ) - jnp.maximum(t - r, 0)).astype(F32)
        z = (acc / cnt - ug).astype(BF16)
        y = jnp.dot(z, w_ref[g], preferred_element_type=F32) * scale_ref[:, cols]
        o_ref[:, cols] = y.astype(BF16)


def _pool(u, pool_w, pool_scale):
    return pl.pallas_call(
        _pool_kernel,
        grid=(BATCH,),
        in_specs=[
            pl.BlockSpec((None, SEQ, POOL_WIDTH), lambda b: (b, 0, 0)),
            pl.BlockSpec((POOL_GROUPS, POOL_GROUP_WIDTH, POOL_GROUP_WIDTH), lambda b: (0, 0, 0)),
            pl.BlockSpec((1, POOL_WIDTH), lambda b: (0, 0)),
        ],
        out_specs=pl.BlockSpec((None, SEQ, POOL_WIDTH), lambda b: (b, 0, 0)),
        out_shape=jax.ShapeDtypeStruct((BATCH, SEQ, POOL_WIDTH), BF16),
        compiler_params=_params(1),
        name="pool",
    )(u, pool_w, pool_scale.reshape(1, POOL_WIDTH))


C_BQ = 256
C_NQ = SEQ // C_BQ
C_BIAS_BLOCKS = 2 * C_NQ - 1
C_Q_SCALE = C_HALF_DIM ** -0.5 * LOG2E


def _attn_c_bias_kernel(slopes_ref, o_ref):
    s = slopes_ref[pl.program_id(0)] * LOG2E
    r = lax.broadcasted_iota(jnp.int32, (C_BQ, C_BQ), 0)
    cc = lax.broadcasted_iota(jnp.int32, (C_BQ, C_BQ), 1)
    for cb in range(C_BIAS_BLOCKS):
        dist = jnp.abs(r - cc - (cb - (C_NQ - 1)) * C_BQ)
        o_ref[cb] = s * dist.astype(F32)


def _attn_c_bias():
    slopes = jnp.asarray(_alibi_slopes(C_HEADS), F32)
    return pl.pallas_call(
        _attn_c_bias_kernel,
        grid=(C_HEADS,),
        in_specs=[pl.BlockSpec(memory_space=pltpu.SMEM)],
        out_specs=pl.BlockSpec((None, C_BIAS_BLOCKS, C_BQ, C_BQ), lambda h: (h, 0, 0, 0)),
        out_shape=jax.ShapeDtypeStruct((C_HEADS, C_BIAS_BLOCKS, C_BQ, C_BQ), F32),
        compiler_params=_params(1),
        name="attn_diff_bias",
    )(slopes)


C_BLOCKS = BATCH * C_HEADS * C_NQ
C_STAGES = 3


def _attn_c_block(n):
    n = jnp.clip(n, 0, C_BLOCKS - 1)
    return n // (C_HEADS * C_NQ), (n // C_NQ) % C_HEADS, n % C_NQ


def _attn_c_step(lam_init, lam_ref, g_ref, bias_ref, q_ref, v_ref, o_ref, kt_ref,
                 qi, logit_w, max_w, logit_r, max_r, prob_w, sum_w, prob_r, sum_r):
    lv = lam_ref[...]
    lam = (jnp.exp(jnp.sum(lv[0:1] * lv[1:2], axis=-1, keepdims=True))
           - jnp.exp(jnp.sum(lv[2:3] * lv[3:4], axis=-1, keepdims=True)) + lam_init)
    v = v_ref[...]
    o1 = jnp.dot(prob_r[0], v, preferred_element_type=F32)
    o2 = jnp.dot(prob_r[1], v, preferred_element_type=F32)
    o = o1 * (1.0 / sum_r[0]) - o2 * (lam / sum_r[1])
    o = _rms(o, g_ref[...], SUBLN_EPS) * (1.0 - lam_init)
    o_ref[...] = o.astype(BF16)

    for half in range(2):
        p = jnp.exp2(logit_r[half] - max_r[half])
        sum_w[half] = jnp.sum(p, axis=-1, keepdims=True)
        prob_w[half] = p.astype(BF16)

    q = q_ref[...]
    lane = lax.broadcasted_iota(jnp.int32, (C_BQ, HEAD_DIM), 1)
    zero = jnp.zeros_like(q)
    kt = kt_ref[...]
    bias = jnp.concatenate(
        [bias_ref[C_NQ - 1 - qi + kb] for kb in range(C_NQ)], axis=1)
    for half, keep in enumerate((lane < C_HALF_DIM, lane >= C_HALF_DIM)):
        logits = jnp.dot(jnp.where(keep, q, zero), kt, preferred_element_type=F32) - bias
        logit_w[half] = logits
        max_w[half] = jnp.max(logits, axis=-1, keepdims=True)


def _attn_c_kernel(lam_init, lam_ref, g_ref, bias_ref, q_ref, k_ref, v_ref, o_ref, kt_ref,
                   logit0, logit1, max0, max1, prob0, prob1, sum0, sum1):
    t = pl.program_id(0)
    qi = jnp.minimum(t, C_BLOCKS - 1) % C_NQ

    @pl.when(t == 0)
    def _():
        logit1[...] = jnp.zeros_like(logit1)
        max1[...] = jnp.zeros_like(max1)
        prob0[...] = jnp.zeros_like(prob0)
        sum0[...] = jnp.ones_like(sum0)

    @pl.when((qi == 0) & (t < C_BLOCKS))
    def _():
        kt_ref[...] = k_ref[...].astype(F32).T.astype(BF16)

    step = functools.partial(_attn_c_step, lam_init, lam_ref, g_ref, bias_ref, q_ref, v_ref,
                             o_ref, kt_ref, qi)

    @pl.when(t % 2 == 0)
    def _():
        step(logit0, max0, logit1, max1, prob1, sum1, prob0, sum0)

    @pl.when(t % 2 == 1)
    def _():
        step(logit1, max1, logit0, max0, prob0, sum0, prob1, sum1)


def _attn_c(proj_c, bias, lam_vecs, subln_g, lam_init):
    col0 = (IN_WIDTH - 3 * C_W) // HEAD_DIM

    def q_index(t):
        b, h, i = _attn_c_block(t)
        return (b, i, col0 + h)

    def k_index(t):
        b, h, _ = _attn_c_block(t)
        return (b, 0, col0 + C_HEADS + h)

    def v_index(t):
        b, h, _ = _attn_c_block(t - (C_STAGES - 1))
        return (b, 0, col0 + 2 * C_HEADS + h)

    def o_index(t):
        b, h, i = _attn_c_block(t - (C_STAGES - 1))
        return (b, i, h)

    half_rows = lambda width, dtype: pltpu.VMEM((2, C_BQ, width), dtype)
    return pl.pallas_call(
        functools.partial(_attn_c_kernel, lam_init),
        grid=(C_BLOCKS + C_STAGES - 1,),
        in_specs=[
            pl.BlockSpec((4, C_HALF_DIM), lambda t: (0, 0)),
            pl.BlockSpec((1, HEAD_DIM), lambda t: (0, 0)),
            pl.BlockSpec((None, C_BIAS_BLOCKS, C_BQ, C_BQ), lambda t: (_attn_c_block(t)[1], 0, 0, 0)),
            pl.BlockSpec((None, C_BQ, HEAD_DIM), q_index),
            pl.BlockSpec((None, SEQ, HEAD_DIM), k_index),
            pl.BlockSpec((None, SEQ, HEAD_DIM), v_index),
        ],
        out_specs=pl.BlockSpec((None, C_BQ, HEAD_DIM), o_index),
        out_shape=jax.ShapeDtypeStruct((BATCH, SEQ, C_W), BF16),
        scratch_shapes=[
            pltpu.VMEM((HEAD_DIM, SEQ), BF16),
            half_rows(SEQ, F32), half_rows(SEQ, F32),
            half_rows(1, F32), half_rows(1, F32),
            half_rows(SEQ, BF16), half_rows(SEQ, BF16),
            half_rows(1, F32), half_rows(1, F32),
        ],
        compiler_params=_params(1),
        name="attn_diff",
    )(lam_vecs, subln_g.reshape(1, HEAD_DIM), bias, proj_c, proj_c, proj_c)


MERGE_BM = 1024
MERGE_BN = 256


def _merge_kernel(h_ref, oa_ref, ob_ref, oc_ref, wg0_ref, wg1_ref, wg2_ref,
                  b0_ref, b1_ref, b2_ref, wa_ref, wb_ref, wc_ref, o_ref):
    h = h_ref[...]

    def branch(wg_ref, b_ref, x_ref, w_ref):
        gate = jax.nn.sigmoid(jnp.dot(h, wg_ref[...], preferred_element_type=F32) + b_ref[...])
        return gate * jnp.dot(x_ref[...], w_ref[...], preferred_element_type=F32)

    merged = (branch(wg0_ref, b0_ref, oa_ref, wa_ref)
              + branch(wg1_ref, b1_ref, ob_ref, wb_ref)
              + branch(wg2_ref, b2_ref, oc_ref, wc_ref))
    o_ref[...] = merged.astype(BF16)


def _merge(h, oa, ob, oc, w_gate, b_gate, w_branch):
    bm, bn = MERGE_BM, MERGE_BN
    nb = D_MODEL // bn
    row = lambda width: pl.BlockSpec((bm, width), lambda i, j: (i, 0))
    gate_w = lambda br: pl.BlockSpec((D_MODEL, bn), lambda i, j: (0, j + br * nb))
    gate_b = lambda br: pl.BlockSpec((1, bn), lambda i, j: (0, j + br * nb))
    branch_w = lambda width, row0: pl.BlockSpec((width, bn), lambda i, j: (row0 // width, j))
    b_gate = b_gate.reshape(1, N_BRANCHES * D_MODEL)
    return pl.pallas_call(
        _merge_kernel,
        grid=(ROWS // bm, nb),
        in_specs=[
            row(D_MODEL), row(A_Q_W), row(POOL_WIDTH), row(C_W),
            gate_w(0), gate_w(1), gate_w(2),
            gate_b(0), gate_b(1), gate_b(2),
            branch_w(A_Q_W, 0), branch_w(POOL_WIDTH, BRANCH_B_ROW), branch_w(C_W, BRANCH_C_ROW),
        ],
        out_specs=pl.BlockSpec((bm, bn), lambda i, j: (i, j)),
        out_shape=jax.ShapeDtypeStruct((ROWS, D_MODEL), BF16),
        compiler_params=_params(2),
        name="gated_merge",
    )(h, oa, ob, oc, w_gate, w_gate, w_gate, b_gate, b_gate, b_gate,
      w_branch, w_branch, w_branch)


FFN_BM = 2048
FFN_BF = 256


def _ffn_up_kernel(h_ref, wg_ref, wu_ref, wd_ref, o_ref, wd_o, wgb_ref, wub_ref):
    @pl.when(pl.program_id(1) == 0)
    def _():
        wgb_ref[...] = wg_ref[...].astype(BF16)
        wub_ref[...] = wu_ref[...].astype(BF16)

    wd_o[...] = wd_ref[...].astype(BF16)
    h = h_ref[...]
    a = jnp.dot(h, wgb_ref[...], preferred_element_type=F32)
    b = jnp.dot(h, wub_ref[...], preferred_element_type=F32)
    o_ref[...] = (a * jax.nn.sigmoid(a) * b).astype(BF16)


def _ffn_up(h, wg, wu, wd, layer):
    bm, bf = FFN_BM, FFN_BF
    steps = ROWS // bm
    wd_cols = D_MODEL // steps
    w_spec = pl.BlockSpec((None, D_MODEL, bf), lambda j, i: (layer, 0, j))
    return pl.pallas_call(
        _ffn_up_kernel,
        grid=(FFN_HIDDEN // bf, steps),
        in_specs=[pl.BlockSpec((bm, D_MODEL), lambda j, i: (i, 0)), w_spec, w_spec,
                  pl.BlockSpec((None, bf, wd_cols), lambda j, i: (layer, j, i))],
        out_specs=[pl.BlockSpec((bm, bf), lambda j, i: (i, j)),
                   pl.BlockSpec((bf, wd_cols), lambda j, i: (j, i))],
        out_shape=[jax.ShapeDtypeStruct((ROWS, FFN_HIDDEN), BF16),
                   jax.ShapeDtypeStruct((FFN_HIDDEN, D_MODEL), BF16)],
        scratch_shapes=[pltpu.VMEM((D_MODEL, bf), BF16), pltpu.VMEM((D_MODEL, bf), BF16)],
        compiler_params=_params(2),
        name="ffn_up",
    )(h, wg, wu, wd)


def kernel(x, c, ada_w, ada_b, mix_pre_g, mix_post_g, w_in, attn_sink, pool_w, pool_scale, diff_lambda, diff_subln_g, w_gate, b_gate, w_branch, w_out, ffn_pre_g, ffn_post_g, ffn_w_gate, ffn_w_up, ffn_w_down):
    c_pad = jnp.pad(c, ((0, ADA_ROWS - BATCH), (0, 0)))
    mod = _ada_mod(c_pad, ada_w, ada_b)[:, :BATCH]
    mod = mod.reshape(DEPTH * BATCH * ADA_CHUNKS, 1, D_MODEL)

    x = x.reshape(ROWS, D_MODEL)
    h = _norm_mod(x, mix_pre_g[0], mod, 0, 0)
    bias_a = _attn_a_bias()
    bias_c = _attn_c_bias()
    for l in range(DEPTH):
        lam_init = 0.8 - 0.6 * math.exp(-0.3 * l)
        proj, proj_u, w_gate_b, w_branch_b = _proj(h, w_in, w_gate, w_branch, l)
        proj = proj.reshape(BATCH, SEQ, IN_WIDTH)

        oa = _attn_a(proj, attn_sink[l], bias_a)
        ob = _pool(proj_u.reshape(BATCH, SEQ, POOL_WIDTH), pool_w[l].astype(BF16), pool_scale[l])
        oc = _attn_c(proj, bias_c, diff_lambda[l], diff_subln_g[l], lam_init)

        merged = _merge(
            h, oa.reshape(ROWS, A_Q_W), ob.reshape(ROWS, POOL_WIDTH), oc.reshape(ROWS, C_W),
            w_gate_b, b_gate[l], w_branch_b)
        y = _matmul_ws(merged, w_out, l, F32, 1024, 512, name="w_out")
        x, h = _resid(x, y, mix_post_g[l], mod, l, 2, nxt=(ffn_pre_g[l], l, 3))

        hid, w_down_b = _ffn_up(h, ffn_w_gate, ffn_w_up, ffn_w_down, l)
        y = _matmul(hid, w_down_b, F32, 512, 512, name="ffn_down")
        if l + 1 < DEPTH:
            x, h = _resid(x, y, ffn_post_g[l], mod, l, 5, nxt=(mix_pre_g[l + 1], l + 1, 0))
        else:
            x = _resid(x, y, ffn_post_g[l], mod, l, 5)
    return x.reshape(BATCH, SEQ, D_MODEL)
```

```python
import functools
import math

import jax
import jax.numpy as jnp
import numpy as np
from jax import lax
from jax.experimental import pallas as pl
from jax.experimental.pallas import tpu as pltpu

D_MODEL = 4096
BATCH = 8
SEQ = 2048
DEPTH = 2
ROWS = BATCH * SEQ

HEAD_DIM = 128
A_HEADS = 12
A_KV_HEADS = 4
A_GROUP = 3
A_WINDOW = 128
A_BLOCK = 128
A_Q_W = A_HEADS * HEAD_DIM
A_KV_W = A_KV_HEADS * HEAD_DIM
POOL_GROUPS = 4
POOL_WIDTH = 1024
POOL_GROUP_WIDTH = 256
POOL_WINDOWS = (2, 4, 8, 16)
C_HEADS = 12
C_HALF_DIM = 64
C_W = C_HEADS * HEAD_DIM
IN_WIDTH = 8192
N_BRANCHES = 3
ADA_CHUNKS = 6
FFN_HIDDEN = 11008
NEG_INF = -1e30
NORM_EPS = 1e-6
SUBLN_EPS = 1e-5

V7X_VMEM_LIMIT_BYTES = 60 * 1024 * 1024

BF16 = jnp.bfloat16
F32 = jnp.float32


def _params(n_grid_dims):
    return pltpu.CompilerParams(
        dimension_semantics=("arbitrary",) * n_grid_dims,
        vmem_limit_bytes=V7X_VMEM_LIMIT_BYTES,
    )


def _alibi_slopes(n):
    return [float(2.0 ** (-8.0 * h / n)) for h in range(1, n + 1)]


ADA_BN = 512
ADA_ROWS = 16


def _ada_kernel(c_ref, w_ref, b_ref, o_ref):
    c = c_ref[...]
    cond = (c * jax.nn.sigmoid(c)).astype(BF16)
    w = w_ref[...].astype(BF16)
    o_ref[...] = jnp.dot(cond, w, preferred_element_type=F32) + b_ref[...]


def _ada_mod(c_pad, ada_w, ada_b):
    n = ADA_CHUNKS * D_MODEL
    return pl.pallas_call(
        _ada_kernel,
        grid=(DEPTH, n // ADA_BN),
        in_specs=[
            pl.BlockSpec((ADA_ROWS, D_MODEL), lambda l, j: (0, 0)),
            pl.BlockSpec((None, D_MODEL, ADA_BN), lambda l, j: (l, 0, j)),
            pl.BlockSpec((None, 1, ADA_BN), lambda l, j: (l, 0, j)),
        ],
        out_specs=pl.BlockSpec((None, ADA_ROWS, ADA_BN), lambda l, j: (l, 0, j)),
        out_shape=jax.ShapeDtypeStruct((DEPTH, ADA_ROWS, n), F32),
        compiler_params=_params(2),
        name="ada_mod",
    )(c_pad, ada_w, ada_b.reshape(DEPTH, 1, n))


NORM_ROWS = 256


def _rms(x, g, eps):
    return x * lax.rsqrt(jnp.mean(x * x, axis=-1, keepdims=True) + eps) * g


def _mod_index(layer, chunk):
    def index(i):
        b = (i * NORM_ROWS) // SEQ
        return ((layer * BATCH + b) * ADA_CHUNKS + chunk, 0, 0)
    return index


def _norm_mod_kernel(x_ref, g_ref, shift_ref, scale_ref, h_ref):
    h = _rms(x_ref[...], g_ref[...], NORM_EPS)
    h_ref[...] = (h * (1.0 + scale_ref[...]) + shift_ref[...]).astype(BF16)


def _norm_mod(x, g, mod, layer, shift_chunk):
    row_spec = pl.BlockSpec((NORM_ROWS, D_MODEL), lambda i: (i, 0))
    vec_spec = pl.BlockSpec((1, D_MODEL), lambda i: (0, 0))
    return pl.pallas_call(
        _norm_mod_kernel,
        grid=(ROWS // NORM_ROWS,),
        in_specs=[
            row_spec, vec_spec,
            pl.BlockSpec((None, 1, D_MODEL), _mod_index(layer, shift_chunk)),
            pl.BlockSpec((None, 1, D_MODEL), _mod_index(layer, shift_chunk + 1)),
        ],
        out_specs=row_spec,
        out_shape=jax.ShapeDtypeStruct((ROWS, D_MODEL), BF16),
        compiler_params=_params(1),
        name="norm_mod",
    )(x, g.reshape(1, D_MODEL), mod, mod)


def _resid_kernel(x_ref, y_ref, gate_ref, post_g_ref, o_ref):
    o_ref[...] = x_ref[...] + gate_ref[...] * _rms(y_ref[...], post_g_ref[...], NORM_EPS)


def _resid_next_kernel(x_ref, y_ref, gate_ref, post_g_ref, pre_g_ref,
                       shift_ref, scale_ref, o_ref, h_ref):
    x_new = x_ref[...] + gate_ref[...] * _rms(y_ref[...], post_g_ref[...], NORM_EPS)
    o_ref[...] = x_new
    h = _rms(x_new, pre_g_ref[...], NORM_EPS)
    h_ref[...] = (h * (1.0 + scale_ref[...]) + shift_ref[...]).astype(BF16)


def _resid(x, y, post_g, mod, layer, gate_chunk, nxt=None):
    row_spec = pl.BlockSpec((NORM_ROWS, D_MODEL), lambda i: (i, 0))
    vec_spec = pl.BlockSpec((1, D_MODEL), lambda i: (0, 0))
    gate_spec = pl.BlockSpec((None, 1, D_MODEL), _mod_index(layer, gate_chunk))
    if nxt is None:
        return pl.pallas_call(
            _resid_kernel,
            grid=(ROWS // NORM_ROWS,),
            in_specs=[row_spec, row_spec, gate_spec, vec_spec],
            out_specs=row_spec,
            out_shape=jax.ShapeDtypeStruct((ROWS, D_MODEL), F32),
            compiler_params=_params(1),
            name="resid",
        )(x, y, mod, post_g.reshape(1, D_MODEL))
    pre_g, nlayer, shift_chunk = nxt
    return pl.pallas_call(
        _resid_next_kernel,
        grid=(ROWS // NORM_ROWS,),
        in_specs=[
            row_spec, row_spec, gate_spec, vec_spec, vec_spec,
            pl.BlockSpec((None, 1, D_MODEL), _mod_index(nlayer, shift_chunk)),
            pl.BlockSpec((None, 1, D_MODEL), _mod_index(nlayer, shift_chunk + 1)),
        ],
        out_specs=[row_spec, row_spec],
        out_shape=[jax.ShapeDtypeStruct((ROWS, D_MODEL), F32),
                   jax.ShapeDtypeStruct((ROWS, D_MODEL), BF16)],
        compiler_params=_params(1),
        name="resid_next",
    )(x, y, mod, post_g.reshape(1, D_MODEL), pre_g.reshape(1, D_MODEL), mod, mod)


def _matmul_kernel(a_ref, w_ref, o_ref):
    o_ref[...] = jnp.dot(a_ref[...], w_ref[...],
                         preferred_element_type=F32).astype(o_ref.dtype)


def _matmul(a, w, out_dtype, bm, bn, n_off=0, n_blocks=None, name="matmul"):
    m, k = a.shape
    if n_blocks is None:
        n_blocks = w.shape[1] // bn - n_off
    return pl.pallas_call(
        _matmul_kernel,
        grid=(m // bm, n_blocks),
        in_specs=[
            pl.BlockSpec((bm, k), lambda i, j: (i, 0)),
            pl.BlockSpec((k, bn), lambda i, j: (0, j + n_off)),
        ],
        out_specs=pl.BlockSpec((bm, bn), lambda i, j: (i, j)),
        out_shape=jax.ShapeDtypeStruct((m, n_blocks * bn), out_dtype),
        compiler_params=_params(2),
        name=name,
    )(a, w)


def _matmul_ws_kernel(scaled_blocks, col_scale, a_ref, w_ref, o_ref, wb_ref):
    @pl.when(pl.program_id(1) == 0)
    def _():
        wb_ref[...] = w_ref[...].astype(BF16)

    acc = jnp.dot(a_ref[...], wb_ref[...], preferred_element_type=F32)
    if scaled_blocks:
        acc = acc * jnp.where(pl.program_id(0) < scaled_blocks, col_scale, 1.0)
    o_ref[...] = acc.astype(o_ref.dtype)


def _matmul_ws(a, w, layer, out_dtype, bm, bn, n_off=0, n_blocks=None,
               scaled_blocks=0, col_scale=1.0, name="matmul_ws"):
    m, k = a.shape
    if n_blocks is None:
        n_blocks = w.shape[2] // bn - n_off
    return pl.pallas_call(
        functools.partial(_matmul_ws_kernel, scaled_blocks, col_scale),
        grid=(n_blocks, m // bm),
        in_specs=[
            pl.BlockSpec((bm, k), lambda j, i: (i, 0)),
            pl.BlockSpec((None, k, bn), lambda j, i: (layer, 0, j + n_off)),
        ],
        out_specs=pl.BlockSpec((bm, bn), lambda j, i: (i, j)),
        out_shape=jax.ShapeDtypeStruct((m, n_blocks * bn), out_dtype),
        scratch_shapes=[pltpu.VMEM((k, bn), BF16)],
        compiler_params=_params(2),
        name=name,
    )(a, w)


PROJ_BM = 1024
PROJ_BN = 512
PROJ_TILES = IN_WIDTH // PROJ_BN
PROJ_STEPS = ROWS // PROJ_BM
PROJ_A_TILES = (A_Q_W + 2 * A_KV_W) // PROJ_BN
PROJ_U_TILES = POOL_WIDTH // PROJ_BN
PROJ_Q_TILES = A_Q_W // PROJ_BN
SIDE_ROWS = D_MODEL // PROJ_STEPS
GATE_SIDE_COLS = N_BRANCHES * D_MODEL // PROJ_TILES
BRANCH_SIDE_COLS = D_MODEL // PROJ_TILES
BRANCH_B_ROW = 2048
BRANCH_C_ROW = 3072
BRANCH_ROWS = BRANCH_C_ROW + C_W


def _proj_kernel(h_ref, w_ref, wg_ref, wbr_ref, p_ref, u_ref, wg_o, wbr_o, wb_scr):
    j = pl.program_id(0)

    @pl.when(pl.program_id(1) == 0)
    def _():
        wb_scr[...] = w_ref[...].astype(BF16)

    wg_o[...] = wg_ref[...].astype(BF16)
    wbr_o[...] = wbr_ref[...].astype(BF16)
    acc = jnp.dot(h_ref[...], wb_scr[...], preferred_element_type=F32)
    c0 = PROJ_A_TILES + PROJ_U_TILES
    scale = jnp.where(j < PROJ_Q_TILES, A_Q_SCALE,
                      jnp.where((j >= c0) & (j < c0 + PROJ_Q_TILES), C_Q_SCALE, 1.0))
    p_ref[...] = (acc * scale).astype(BF16)

    @pl.when((j >= PROJ_A_TILES) & (j < c0))
    def _():
        u_ref[...] = acc


def _proj(h, w_in, w_gate, w_branch, layer):
    bm, bn = PROJ_BM, PROJ_BN
    last = PROJ_STEPS - 1
    c0 = PROJ_A_TILES + PROJ_U_TILES
    u_index = lambda j, i: (jnp.where(j < PROJ_A_TILES, 0, jnp.where(j < c0, i, last)),
                            jnp.clip(j - PROJ_A_TILES, 0, PROJ_U_TILES - 1))
    branch_row = lambda i: i + (BRANCH_B_ROW - A_Q_W) // SIDE_ROWS * (i >= A_Q_W // SIDE_ROWS)
    return pl.pallas_call(
        _proj_kernel,
        grid=(PROJ_TILES, PROJ_STEPS),
        in_specs=[
            pl.BlockSpec((bm, D_MODEL), lambda j, i: (i, 0)),
            pl.BlockSpec((None, D_MODEL, bn), lambda j, i: (layer, 0, j)),
            pl.BlockSpec((None, SIDE_ROWS, GATE_SIDE_COLS), lambda j, i: (layer, i, j)),
            pl.BlockSpec((None, SIDE_ROWS, BRANCH_SIDE_COLS), lambda j, i: (layer, i, j)),
        ],
        out_specs=[
            pl.BlockSpec((bm, bn), lambda j, i: (i, j)),
            pl.BlockSpec((bm, bn), u_index),
            pl.BlockSpec((SIDE_ROWS, GATE_SIDE_COLS), lambda j, i: (i, j)),
            pl.BlockSpec((SIDE_ROWS, BRANCH_SIDE_COLS), lambda j, i: (branch_row(i), j)),
        ],
        out_shape=[
            jax.ShapeDtypeStruct((ROWS, IN_WIDTH), BF16),
            jax.ShapeDtypeStruct((ROWS, PROJ_U_TILES * bn), F32),
            jax.ShapeDtypeStruct((D_MODEL, N_BRANCHES * D_MODEL), BF16),
            jax.ShapeDtypeStruct((BRANCH_ROWS, D_MODEL), BF16),
        ],
        scratch_shapes=[pltpu.VMEM((D_MODEL, bn), BF16)],
        compiler_params=_params(2),
        name="proj",
    )(h, w_in, w_gate, w_branch)


A_KEYS = 3 * A_BLOCK
A_NB = SEQ // A_BLOCK
A_GROUP_ROWS = A_GROUP * A_BLOCK
LOG2E = math.log2(math.e)
A_Q_SCALE = HEAD_DIM ** -0.5 * LOG2E


def _attn_a_bias_kernel(o_ref):
    p = pl.program_id(0)
    r = lax.broadcasted_iota(jnp.int32, (A_BLOCK, A_KEYS), 0)
    c = lax.broadcasted_iota(jnp.int32, (A_BLOCK, A_KEYS), 1)
    dist = jnp.abs(r + p * A_BLOCK - c)
    distf = dist.astype(F32)
    slopes = _alibi_slopes(A_HEADS)
    for g in range(A_KV_HEADS):
        for j in range(A_GROUP):
            bias = jnp.where(dist <= A_WINDOW, (-slopes[g * A_GROUP + j] * LOG2E) * distf, NEG_INF)
            o_ref[g, j * A_BLOCK:(j + 1) * A_BLOCK, :] = bias


def _attn_a_bias():
    return pl.pallas_call(
        _attn_a_bias_kernel,
        grid=(3,),
        out_specs=pl.BlockSpec((None, A_KV_HEADS, A_GROUP_ROWS, A_KEYS), lambda p: (p, 0, 0, 0)),
        out_shape=jax.ShapeDtypeStruct((3, A_KV_HEADS, A_GROUP_ROWS, A_KEYS), F32),
        compiler_params=_params(1),
        name="attn_window_bias",
    )()


def _attn_a_kernel(sink_ref, bias_ref, q_ref, k_ref, v_ref, o_ref):
    n = pl.program_id(1)
    start = jnp.clip(n * A_BLOCK - A_BLOCK, 0, SEQ - A_KEYS)
    start = pl.multiple_of(start, A_BLOCK)
    nt = (((1,), (1,)), ((), ()))
    logits = []
    for g in range(A_KV_HEADS):
        k_g = k_ref[pl.ds(start, A_KEYS), g * HEAD_DIM:(g + 1) * HEAD_DIM]
        q_g = jnp.concatenate(
            [q_ref[:, h * HEAD_DIM:(h + 1) * HEAD_DIM]
             for h in range(g * A_GROUP, (g + 1) * A_GROUP)], axis=0)
        logits.append(lax.dot_general(q_g, k_g, nt, preferred_element_type=F32) + bias_ref[g])
    logits = jnp.concatenate(logits, axis=0)
    sink = jnp.concatenate(
        [jnp.full((A_BLOCK, HEAD_DIM), sink_ref[h] * LOG2E, F32) for h in range(A_HEADS)], axis=0)
    m = jnp.maximum(jnp.broadcast_to(jnp.max(logits, axis=-1, keepdims=True), sink.shape), sink)
    p = jnp.exp2(logits - jnp.concatenate([m] * (A_KEYS // HEAD_DIM), axis=1)).astype(BF16)
    sink_term = jnp.exp2(sink - m)
    ones = jnp.ones((A_KEYS, HEAD_DIM), BF16)
    for g in range(A_KV_HEADS):
        rows = slice(g * A_GROUP_ROWS, (g + 1) * A_GROUP_ROWS)
        v_g = v_ref[pl.ds(start, A_KEYS), g * HEAD_DIM:(g + 1) * HEAD_DIM]
        pv = jnp.dot(p[rows], jnp.concatenate([v_g, ones], axis=1), preferred_element_type=F32)
        o = pv[:, :HEAD_DIM] * (1.0 / (pv[:, HEAD_DIM:] + sink_term[rows]))
        for j in range(A_GROUP):
            h = g * A_GROUP + j
            o_ref[:, h * HEAD_DIM:(h + 1) * HEAD_DIM] = o[j * A_BLOCK:(j + 1) * A_BLOCK].astype(BF16)


def _attn_a(proj_a, sink, bias):
    pattern = lambda n: jnp.minimum(n, 1) + jnp.maximum(n - (A_NB - 2), 0)
    return pl.pallas_call(
        _attn_a_kernel,
        grid=(BATCH, A_NB),
        in_specs=[
            pl.BlockSpec(memory_space=pltpu.SMEM),
            pl.BlockSpec((None, A_KV_HEADS, A_GROUP_ROWS, A_KEYS), lambda b, n: (pattern(n), 0, 0, 0)),
            pl.BlockSpec((None, A_BLOCK, A_Q_W), lambda b, n: (b, n, 0)),
            pl.BlockSpec((None, SEQ, A_KV_W), lambda b, n: (b, 0, A_Q_W // A_KV_W)),
            pl.BlockSpec((None, SEQ, A_KV_W), lambda b, n: (b, 0, A_Q_W // A_KV_W + 1)),
        ],
        out_specs=pl.BlockSpec((None, A_BLOCK, A_Q_W), lambda b, n: (b, n, 0)),
        out_shape=jax.ShapeDtypeStruct((BATCH, SEQ, A_Q_W), BF16),
        compiler_params=_params(2),
        name="attn_window",
    )(sink, bias, proj_a, proj_a, proj_a)


def _pool_kernel(u_ref, w_ref, scale_ref, o_ref):
    t = lax.broadcasted_iota(jnp.int32, (SEQ, 1), 0)
    for g, win in enumerate(POOL_WINDOWS):
        r = win // 2
        cols = slice(g * POOL_GROUP_WIDTH, (g + 1) * POOL_GROUP_WIDTH)
        ug = u_ref[:, cols]
        acc = ug
        for d in range(1, r + 1):
            below = pltpu.roll(ug, d, 0)
            above = pltpu.roll(ug, SEQ - d, 0)
            acc = acc + jnp.where(t >= d, below, 0.0) + jnp.where(t < SEQ - d, above, 0.0)
        cnt = (jnp.minimum(t + r + 1, SEQ) - jnp.maximum(t - r, 0)).astype(F32)
        z = (acc / cnt - ug).astype(BF16)
        y = jnp.dot(z, w_ref[g], preferred_element_type=F32) * scale_ref[:, cols]
        o_ref[:, cols] = y.astype(BF16)


def _pool(u, pool_w, pool_scale):
    return pl.pallas_call(
        _pool_kernel,
        grid=(BATCH,),
        in_specs=[
            pl.BlockSpec((None, SEQ, POOL_WIDTH), lambda b: (b, 0, 0)),
            pl.BlockSpec((POOL_GROUPS, POOL_GROUP_WIDTH, POOL_GROUP_WIDTH), lambda b: (0, 0, 0)),
            pl.BlockSpec((1, POOL_WIDTH), lambda b: (0, 0)),
        ],
        out_specs=pl.BlockSpec((None, SEQ, POOL_WIDTH), lambda b: (b, 0, 0)),
        out_shape=jax.ShapeDtypeStruct((BATCH, SEQ, POOL_WIDTH), BF16),
        compiler_params=_params(1),
        name="pool",
    )(u, pool_w, pool_scale.reshape(1, POOL_WIDTH))


C_BQ = 512
C_NQ = SEQ // C_BQ
C_BIAS_BLOCKS = 2 * C_NQ - 1
C_Q_SCALE = C_HALF_DIM ** -0.5 * LOG2E


def _attn_c_bias_kernel(slopes_ref, o_ref):
    s = slopes_ref[pl.program_id(0)] * LOG2E
    r = lax.broadcasted_iota(jnp.int32, (C_BQ, C_BQ), 0)
    cc = lax.broadcasted_iota(jnp.int32, (C_BQ, C_BQ), 1)
    for cb in range(C_BIAS_BLOCKS):
        dist = jnp.abs(r - cc - (cb - (C_NQ - 1)) * C_BQ)
        o_ref[cb] = s * dist.astype(F32)


def _attn_c_bias():
    slopes = jnp.asarray(_alibi_slopes(C_HEADS), F32)
    return pl.pallas_call(
        _attn_c_bias_kernel,
        grid=(C_HEADS,),
        in_specs=[pl.BlockSpec(memory_space=pltpu.SMEM)],
        out_specs=pl.BlockSpec((None, C_BIAS_BLOCKS, C_BQ, C_BQ), lambda h: (h, 0, 0, 0)),
        out_shape=jax.ShapeDtypeStruct((C_HEADS, C_BIAS_BLOCKS, C_BQ, C_BQ), F32),
        compiler_params=_params(1),
        name="attn_diff_bias",
    )(slopes)


C_BLOCKS = BATCH * C_HEADS * C_NQ
C_STAGES = 3


def _attn_c_block(n):
    n = jnp.clip(n, 0, C_BLOCKS - 1)
    return n // (C_HEADS * C_NQ), (n // C_NQ) % C_HEADS, n % C_NQ


def _attn_c_step(lam_init, lam_ref, g_ref, bias_ref, q_ref, v_ref, o_ref, kt_ref,
                 qi, logit_w, max_w, logit_r, max_r, prob_w, sum_w, prob_r, sum_r):
    lv = lam_ref[...]
    lam = (jnp.exp(jnp.sum(lv[0:1] * lv[1:2], axis=-1, keepdims=True))
           - jnp.exp(jnp.sum(lv[2:3] * lv[3:4], axis=-1, keepdims=True)) + lam_init)
    v = v_ref[...]
    o1 = jnp.dot(prob_r[0], v, preferred_element_type=F32)
    o2 = jnp.dot(prob_r[1], v, preferred_element_type=F32)
    o = o1 * (1.0 / sum_r[0]) - o2 * (lam / sum_r[1])
    o = _rms(o, g_ref[...], SUBLN_EPS) * (1.0 - lam_init)
    o_ref[...] = o.astype(BF16)

    for half in range(2):
        p = jnp.exp2(logit_r[half] - max_r[half])
        sum_w[half] = jnp.sum(p, axis=-1, keepdims=True)
        prob_w[half] = p.astype(BF16)

    q = q_ref[...]
    lane = lax.broadcasted_iota(jnp.int32, (C_BQ, HEAD_DIM), 1)
    zero = jnp.zeros_like(q)
    kt = kt_ref[...]
    bias = jnp.concatenate(
        [bias_ref[C_NQ - 1 - qi + kb] for kb in range(C_NQ)], axis=1)
    for half, keep in enumerate((lane < C_HALF_DIM, lane >= C_HALF_DIM)):
        logits = jnp.dot(jnp.where(keep, q, zero), kt, preferred_element_type=F32) - bias
        logit_w[half] = logits
        max_w[half] = jnp.max(logits, axis=-1, keepdims=True)


def _attn_c_kernel(lam_init, lam_ref, g_ref, bias_ref, q_ref, k_ref, v_ref, o_ref, kt_ref,
                   logit0, logit1, max0, max1, prob0, prob1, sum0, sum1):
    t = pl.program_id(0)
    qi = jnp.minimum(t, C_BLOCKS - 1) % C_NQ

    @pl.when(t == 0)
    def _():
        logit1[...] = jnp.zeros_like(logit1)
        max1[...] = jnp.zeros_like(max1)
        prob0[...] = jnp.zeros_like(prob0)
        sum0[...] = jnp.ones_like(sum0)

    @pl.when((qi == 0) & (t < C_BLOCKS))
    def _():
        kt_ref[...] = k_ref[...].astype(F32).T.astype(BF16)

    step = functools.partial(_attn_c_step, lam_init, lam_ref, g_ref, bias_ref, q_ref, v_ref,
                             o_ref, kt_ref, qi)

    @pl.when(t % 2 == 0)
    def _():
        step(logit0, max0, logit1, max1, prob1, sum1, prob0, sum0)

    @pl.when(t % 2 == 1)
    def _():
        step(logit1, max1, logit0, max0, prob0, sum0, prob1, sum1)


def _attn_c(proj_c, bias, lam_vecs, subln_g, lam_init):
    col0 = (IN_WIDTH - 3 * C_W) // HEAD_DIM

    def q_index(t):
        b, h, i = _attn_c_block(t)
        return (b, i, col0 + h)

    def k_index(t):
        b, h, _ = _attn_c_block(t)
        return (b, 0, col0 + C_HEADS + h)

    def v_index(t):
        b, h, _ = _attn_c_block(t - (C_STAGES - 1))
        return (b, 0, col0 + 2 * C_HEADS + h)

    def o_index(t):
        b, h, i = _attn_c_block(t - (C_STAGES - 1))
        return (b, i, h)

    half_rows = lambda width, dtype: pltpu.VMEM((2, C_BQ, width), dtype)
    return pl.pallas_call(
        functools.partial(_attn_c_kernel, lam_init),
        grid=(C_BLOCKS + C_STAGES - 1,),
        in_specs=[
            pl.BlockSpec((4, C_HALF_DIM), lambda t: (0, 0)),
            pl.BlockSpec((1, HEAD_DIM), lambda t: (0, 0)),
            pl.BlockSpec((None, C_BIAS_BLOCKS, C_BQ, C_BQ), lambda t: (_attn_c_block(t)[1], 0, 0, 0)),
            pl.BlockSpec((None, C_BQ, HEAD_DIM), q_index),
            pl.BlockSpec((None, SEQ, HEAD_DIM), k_index),
            pl.BlockSpec((None, SEQ, HEAD_DIM), v_index),
        ],
        out_specs=pl.BlockSpec((None, C_BQ, HEAD_DIM), o_index),
        out_shape=jax.ShapeDtypeStruct((BATCH, SEQ, C_W), BF16),
        scratch_shapes=[
            pltpu.VMEM((HEAD_DIM, SEQ), BF16),
            half_rows(SEQ, F32), half_rows(SEQ, F32),
            half_rows(1, F32), half_rows(1, F32),
            half_rows(SEQ, BF16), half_rows(SEQ, BF16),
            half_rows(1, F32), half_rows(1, F32),
        ],
        compiler_params=_params(1),
        name="attn_diff",
    )(lam_vecs, subln_g.reshape(1, HEAD_DIM), bias, proj_c, proj_c, proj_c)


MERGE_BM = 1024
MERGE_BN = 256


def _merge_kernel(h_ref, oa_ref, ob_ref, oc_ref, wg0_ref, wg1_ref, wg2_ref,
                  b0_ref, b1_ref, b2_ref, wa_ref, wb_ref, wc_ref, o_ref):
    h = h_ref[...]

    def branch(wg_ref, b_ref, x_ref, w_ref):
        gate = jax.nn.sigmoid(jnp.dot(h, wg_ref[...], preferred_element_type=F32) + b_ref[...])
        return gate * jnp.dot(x_ref[...], w_ref[...], preferred_element_type=F32)

    merged = (branch(wg0_ref, b0_ref, oa_ref, wa_ref)
              + branch(wg1_ref, b1_ref, ob_ref, wb_ref)
              + branch(wg2_ref, b2_ref, oc_ref, wc_ref))
    o_ref[...] = merged.astype(BF16)


def _merge(h, oa, ob, oc, w_gate, b_gate, w_branch):
    bm, bn = MERGE_BM, MERGE_BN
    nb = D_MODEL // bn
    row = lambda width: pl.BlockSpec((bm, width), lambda i, j: (i, 0))
    gate_w = lambda br: pl.BlockSpec((D_MODEL, bn), lambda i, j: (0, j + br * nb))
    gate_b = lambda br: pl.BlockSpec((1, bn), lambda i, j: (0, j + br * nb))
    branch_w = lambda width, row0: pl.BlockSpec((width, bn), lambda i, j: (row0 // width, j))
    b_gate = b_gate.reshape(1, N_BRANCHES * D_MODEL)
    return pl.pallas_call(
        _merge_kernel,
        grid=(ROWS // bm, nb),
        in_specs=[
            row(D_MODEL), row(A_Q_W), row(POOL_WIDTH), row(C_W),
            gate_w(0), gate_w(1), gate_w(2),
            gate_b(0), gate_b(1), gate_b(2),
            branch_w(A_Q_W, 0), branch_w(POOL_WIDTH, BRANCH_B_ROW), branch_w(C_W, BRANCH_C_ROW),
        ],
        out_specs=pl.BlockSpec((bm, bn), lambda i, j: (i, j)),
        out_shape=jax.ShapeDtypeStruct((ROWS, D_MODEL), BF16),
        compiler_params=_params(2),
        name="gated_merge",
    )(h, oa, ob, oc, w_gate, w_gate, w_gate, b_gate, b_gate, b_gate,
      w_branch, w_branch, w_branch)


FFN_BM = 2048
FFN_BF = 256


def _ffn_up_kernel(h_ref, wg_ref, wu_ref, wd_ref, o_ref, wd_o, wgb_ref, wub_ref):
    @pl.when(pl.program_id(1) == 0)
    def _():
        wgb_ref[...] = wg_ref[...].astype(BF16)
        wub_ref[...] = wu_ref[...].astype(BF16)

    wd_o[...] = wd_ref[...].astype(BF16)
    h = h_ref[...]
    a = jnp.dot(h, wgb_ref[...], preferred_element_type=F32)
    b = jnp.dot(h, wub_ref[...], preferred_element_type=F32)
    o_ref[...] = (a * jax.nn.sigmoid(a) * b).astype(BF16)


def _ffn_up(h, wg, wu, wd, layer):
    bm, bf = FFN_BM, FFN_BF
    steps = ROWS // bm
    wd_cols = D_MODEL // steps
    w_spec = pl.BlockSpec((None, D_MODEL, bf), lambda j, i: (layer, 0, j))
    return pl.pallas_call(
        _ffn_up_kernel,
        grid=(FFN_HIDDEN // bf, steps),
        in_specs=[pl.BlockSpec((bm, D_MODEL), lambda j, i: (i, 0)), w_spec, w_spec,
                  pl.BlockSpec((None, bf, wd_cols), lambda j, i: (layer, j, i))],
        out_specs=[pl.BlockSpec((bm, bf), lambda j, i: (i, j)),
                   pl.BlockSpec((bf, wd_cols), lambda j, i: (j, i))],
        out_shape=[jax.ShapeDtypeStruct((ROWS, FFN_HIDDEN), BF16),
                   jax.ShapeDtypeStruct((FFN_HIDDEN, D_MODEL), BF16)],
        scratch_shapes=[pltpu.VMEM((D_MODEL, bf), BF16), pltpu.VMEM((D_MODEL, bf), BF16)],
        compiler_params=_params(2),
        name="ffn_up",
    )(h, wg, wu, wd)


def kernel(x, c, ada_w, ada_b, mix_pre_g, mix_post_g, w_in, attn_sink, pool_w, pool_scale, diff_lambda, diff_subln_g, w_gate, b_gate, w_branch, w_out, ffn_pre_g, ffn_post_g, ffn_w_gate, ffn_w_up, ffn_w_down):
    c_pad = jnp.pad(c, ((0, ADA_ROWS - BATCH), (0, 0)))
    mod = _ada_mod(c_pad, ada_w, ada_b)[:, :BATCH]
    mod = mod.reshape(DEPTH * BATCH * ADA_CHUNKS, 1, D_MODEL)

    x = x.reshape(ROWS, D_MODEL)
    h = _norm_mod(x, mix_pre_g[0], mod, 0, 0)
    bias_a = _attn_a_bias()
    bias_c = _attn_c_bias()
    for l in range(DEPTH):
        lam_init = 0.8 - 0.6 * math.exp(-0.3 * l)
        proj, proj_u, w_gate_b, w_branch_b = _proj(h, w_in, w_gate, w_branch, l)
        proj = proj.reshape(BATCH, SEQ, IN_WIDTH)

        oa = _attn_a(proj, attn_sink[l], bias_a)
        ob = _pool(proj_u.reshape(BATCH, SEQ, POOL_WIDTH), pool_w[l].astype(BF16), pool_scale[l])
        oc = _attn_c(proj, bias_c, diff_lambda[l], diff_subln_g[l], lam_init)

        merged = _merge(
            h, oa.reshape(ROWS, A_Q_W), ob.reshape(ROWS, POOL_WIDTH), oc.reshape(ROWS, C_W),
            w_gate_b, b_gate[l], w_branch_b)
        y = _matmul_ws(merged, w_out, l, F32, 1024, 512, name="w_out")
        x, h = _resid(x, y, mix_post_g[l], mod, l, 2, nxt=(ffn_pre_g[l], l, 3))

        hid, w_down_b = _ffn_up(h, ffn_w_gate, ffn_w_up, ffn_w_down, l)
        y = _matmul(hid, w_down_b, F32, 512, 512, name="ffn_down")
        if l + 1 < DEPTH:
            x, h = _resid(x, y, ffn_post_g[l], mod, l, 5, nxt=(mix_pre_g[l + 1], l + 1, 0))
        else:
            x = _resid(x, y, ffn_post_g[l], mod, l, 5)
    return x.reshape(BATCH, SEQ, D_MODEL)
```

```python
import functools
import math

import jax
import jax.numpy as jnp
import numpy as np
from jax import lax
from jax.experimental import pallas as pl
from jax.experimental.pallas import tpu as pltpu

D_MODEL = 4096
BATCH = 8
SEQ = 2048
DEPTH = 2
ROWS = BATCH * SEQ

HEAD_DIM = 128
A_HEADS = 12
A_KV_HEADS = 4
A_GROUP = 3
A_WINDOW = 128
A_BLOCK = 128
A_Q_W = A_HEADS * HEAD_DIM
A_KV_W = A_KV_HEADS * HEAD_DIM
POOL_GROUPS = 4
POOL_WIDTH = 1024
POOL_GROUP_WIDTH = 256
POOL_WINDOWS = (2, 4, 8, 16)
C_HEADS = 12
C_HALF_DIM = 64
C_W = C_HEADS * HEAD_DIM
IN_WIDTH = 8192
N_BRANCHES = 3
ADA_CHUNKS = 6
FFN_HIDDEN = 11008
NEG_INF = -1e30
NORM_EPS = 1e-6
SUBLN_EPS = 1e-5

V7X_VMEM_LIMIT_BYTES = 60 * 1024 * 1024

BF16 = jnp.bfloat16
F32 = jnp.float32


def _params(n_grid_dims):
    return pltpu.CompilerParams(
        dimension_semantics=("arbitrary",) * n_grid_dims,
        vmem_limit_bytes=V7X_VMEM_LIMIT_BYTES,
    )


def _alibi_slopes(n):
    return [float(2.0 ** (-8.0 * h / n)) for h in range(1, n + 1)]


ADA_BN = 512
ADA_ROWS = 16


def _ada_kernel(c_ref, w_ref, b_ref, o_ref):
    c = c_ref[...]
    cond = (c * jax.nn.sigmoid(c)).astype(BF16)
    w = w_ref[...].astype(BF16)
    o_ref[...] = jnp.dot(cond, w, preferred_element_type=F32) + b_ref[...]


def _ada_mod(c_pad, ada_w, ada_b):
    n = ADA_CHUNKS * D_MODEL
    return pl.pallas_call(
        _ada_kernel,
        grid=(DEPTH, n // ADA_BN),
        in_specs=[
            pl.BlockSpec((ADA_ROWS, D_MODEL), lambda l, j: (0, 0)),
            pl.BlockSpec((None, D_MODEL, ADA_BN), lambda l, j: (l, 0, j)),
            pl.BlockSpec((None, 1, ADA_BN), lambda l, j: (l, 0, j)),
        ],
        out_specs=pl.BlockSpec((None, ADA_ROWS, ADA_BN), lambda l, j: (l, 0, j)),
        out_shape=jax.ShapeDtypeStruct((DEPTH, ADA_ROWS, n), F32),
        compiler_params=_params(2),
        name="ada_mod",
    )(c_pad, ada_w, ada_b.reshape(DEPTH, 1, n))


NORM_ROWS = 256


def _rms(x, g, eps):
    return x * lax.rsqrt(jnp.mean(x * x, axis=-1, keepdims=True) + eps) * g


def _mod_index(layer, chunk):
    def index(i):
        b = (i * NORM_ROWS) // SEQ
        return ((layer * BATCH + b) * ADA_CHUNKS + chunk, 0, 0)
    return index


def _norm_mod_kernel(x_ref, g_ref, shift_ref, scale_ref, h_ref):
    h = _rms(x_ref[...], g_ref[...], NORM_EPS)
    h_ref[...] = (h * (1.0 + scale_ref[...]) + shift_ref[...]).astype(BF16)


def _norm_mod(x, g, mod, layer, shift_chunk):
    row_spec = pl.BlockSpec((NORM_ROWS, D_MODEL), lambda i: (i, 0))
    vec_spec = pl.BlockSpec((1, D_MODEL), lambda i: (0, 0))
    return pl.pallas_call(
        _norm_mod_kernel,
        grid=(ROWS // NORM_ROWS,),
        in_specs=[
            row_spec, vec_spec,
            pl.BlockSpec((None, 1, D_MODEL), _mod_index(layer, shift_chunk)),
            pl.BlockSpec((None, 1, D_MODEL), _mod_index(layer, shift_chunk + 1)),
        ],
        out_specs=row_spec,
        out_shape=jax.ShapeDtypeStruct((ROWS, D_MODEL), BF16),
        compiler_params=_params(1),
        name="norm_mod",
    )(x, g.reshape(1, D_MODEL), mod, mod)


def _resid_kernel(x_ref, y_ref, gate_ref, post_g_ref, o_ref):
    o_ref[...] = x_ref[...] + gate_ref[...] * _rms(y_ref[...], post_g_ref[...], NORM_EPS)


def _resid_next_kernel(x_ref, y_ref, gate_ref, post_g_ref, pre_g_ref,
                       shift_ref, scale_ref, o_ref, h_ref):
    x_new = x_ref[...] + gate_ref[...] * _rms(y_ref[...], post_g_ref[...], NORM_EPS)
    o_ref[...] = x_new
    h = _rms(x_new, pre_g_ref[...], NORM_EPS)
    h_ref[...] = (h * (1.0 + scale_ref[...]) + shift_ref[...]).astype(BF16)


def _resid(x, y, post_g, mod, layer, gate_chunk, nxt=None):
    row_spec = pl.BlockSpec((NORM_ROWS, D_MODEL), lambda i: (i, 0))
    vec_spec = pl.BlockSpec((1, D_MODEL), lambda i: (0, 0))
    gate_spec = pl.BlockSpec((None, 1, D_MODEL), _mod_index(layer, gate_chunk))
    if nxt is None:
        return pl.pallas_call(
            _resid_kernel,
            grid=(ROWS // NORM_ROWS,),
            in_specs=[row_spec, row_spec, gate_spec, vec_spec],
            out_specs=row_spec,
            out_shape=jax.ShapeDtypeStruct((ROWS, D_MODEL), F32),
            compiler_params=_params(1),
            name="resid",
        )(x, y, mod, post_g.reshape(1, D_MODEL))
    pre_g, nlayer, shift_chunk = nxt
    return pl.pallas_call(
        _resid_next_kernel,
        grid=(ROWS // NORM_ROWS,),
        in_specs=[
            row_spec, row_spec, gate_spec, vec_spec, vec_spec,
            pl.BlockSpec((None, 1, D_MODEL), _mod_index(nlayer, shift_chunk)),
            pl.BlockSpec((None, 1, D_MODEL), _mod_index(nlayer, shift_chunk + 1)),
        ],
        out_specs=[row_spec, row_spec],
        out_shape=[jax.ShapeDtypeStruct((ROWS, D_MODEL), F32),
                   jax.ShapeDtypeStruct((ROWS, D_MODEL), BF16)],
        compiler_params=_params(1),
        name="resid_next",
    )(x, y, mod, post_g.reshape(1, D_MODEL), pre_g.reshape(1, D_MODEL), mod, mod)


OUT_BM = 512
OUT_BN = 1024
OUT_ROW_TILES = ROWS // OUT_BM
OUT_COL_TILES = D_MODEL // OUT_BN
OUT_SLICE = OUT_BM // OUT_COL_TILES


def _out_resid_step(a_ref, w_ref, x_ref, gate_ref, post_g_ref, pre_g_ref, shift_ref, scale_ref,
                    o_ref, h_ref, y_w, y_r):
    j = pl.program_id(1)
    rows = pl.ds(pl.multiple_of(j * OUT_SLICE, OUT_SLICE), OUT_SLICE)
    y = jnp.concatenate([y_r[n, rows, :] for n in range(OUT_COL_TILES)], axis=1)
    x_new = x_ref[...] + gate_ref[...] * _rms(y, post_g_ref[...], NORM_EPS)
    o_ref[...] = x_new
    h = _rms(x_new, pre_g_ref[...], NORM_EPS)
    h_ref[...] = (h * (1.0 + scale_ref[...]) + shift_ref[...]).astype(BF16)
    y_w[j] = jnp.dot(a_ref[...], w_ref[...], preferred_element_type=F32)


def _out_resid_kernel(a_ref, w_ref, x_ref, gate_ref, post_g_ref, pre_g_ref, shift_ref, scale_ref,
                      o_ref, h_ref, y0, y1):
    i = pl.program_id(0)

    @pl.when((i == 0) & (pl.program_id(1) == 0))
    def _():
        y1[...] = jnp.zeros_like(y1)

    step = functools.partial(_out_resid_step, a_ref, w_ref, x_ref, gate_ref, post_g_ref,
                             pre_g_ref, shift_ref, scale_ref, o_ref, h_ref)

    @pl.when(i % 2 == 0)
    def _():
        step(y0, y1)

    @pl.when(i % 2 == 1)
    def _():
        step(y1, y0)


def _out_resid(a, w, x, post_g, mod, layer, gate_chunk, pre_g, nlayer, shift_chunk):
    prev = lambda i: jnp.maximum(i - 1, 0)
    slice_index = lambda i, j: (prev(i) * OUT_COL_TILES + j, 0)

    def mod_spec(mlayer, chunk):
        def index(i, j):
            b = (prev(i) * OUT_BM) // SEQ
            return ((mlayer * BATCH + b) * ADA_CHUNKS + chunk, 0, 0)
        return pl.BlockSpec((None, 1, D_MODEL), index)

    slice_spec = pl.BlockSpec((OUT_SLICE, D_MODEL), slice_index)
    vec_spec = pl.BlockSpec((1, D_MODEL), lambda i, j: (0, 0))
    y_buf = pltpu.VMEM((OUT_COL_TILES, OUT_BM, OUT_BN), F32)
    return pl.pallas_call(
        _out_resid_kernel,
        grid=(OUT_ROW_TILES + 1, OUT_COL_TILES),
        in_specs=[
            pl.BlockSpec((OUT_BM, D_MODEL), lambda i, j: (jnp.minimum(i, OUT_ROW_TILES - 1), 0)),
            pl.BlockSpec((D_MODEL, OUT_BN), lambda i, j: (0, j)),
            slice_spec, mod_spec(layer, gate_chunk), vec_spec, vec_spec,
            mod_spec(nlayer, shift_chunk), mod_spec(nlayer, shift_chunk + 1),
        ],
        out_specs=[slice_spec, slice_spec],
        out_shape=[jax.ShapeDtypeStruct((ROWS, D_MODEL), F32),
                   jax.ShapeDtypeStruct((ROWS, D_MODEL), BF16)],
        scratch_shapes=[y_buf, y_buf],
        compiler_params=_params(2),
        name="out_resid",
    )(a, w, x, mod, post_g.reshape(1, D_MODEL), pre_g.reshape(1, D_MODEL), mod, mod)


def _matmul_kernel(a_ref, w_ref, o_ref):
    o_ref[...] = jnp.dot(a_ref[...], w_ref[...],
                         preferred_element_type=F32).astype(o_ref.dtype)


def _matmul(a, w, out_dtype, bm, bn, n_off=0, n_blocks=None, name="matmul"):
    m, k = a.shape
    if n_blocks is None:
        n_blocks = w.shape[1] // bn - n_off
    return pl.pallas_call(
        _matmul_kernel,
        grid=(m // bm, n_blocks),
        in_specs=[
            pl.BlockSpec((bm, k), lambda i, j: (i, 0)),
            pl.BlockSpec((k, bn), lambda i, j: (0, j + n_off)),
        ],
        out_specs=pl.BlockSpec((bm, bn), lambda i, j: (i, j)),
        out_shape=jax.ShapeDtypeStruct((m, n_blocks * bn), out_dtype),
        compiler_params=_params(2),
        name=name,
    )(a, w)


def _matmul_ws_kernel(scaled_blocks, col_scale, a_ref, w_ref, o_ref, wb_ref):
    @pl.when(pl.program_id(1) == 0)
    def _():
        wb_ref[...] = w_ref[...].astype(BF16)

    acc = jnp.dot(a_ref[...], wb_ref[...], preferred_element_type=F32)
    if scaled_blocks:
        acc = acc * jnp.where(pl.program_id(0) < scaled_blocks, col_scale, 1.0)
    o_ref[...] = acc.astype(o_ref.dtype)


def _matmul_ws(a, w, layer, out_dtype, bm, bn, n_off=0, n_blocks=None,
               scaled_blocks=0, col_scale=1.0, name="matmul_ws"):
    m, k = a.shape
    if n_blocks is None:
        n_blocks = w.shape[2] // bn - n_off
    return pl.pallas_call(
        functools.partial(_matmul_ws_kernel, scaled_blocks, col_scale),
        grid=(n_blocks, m // bm),
        in_specs=[
            pl.BlockSpec((bm, k), lambda j, i: (i, 0)),
            pl.BlockSpec((None, k, bn), lambda j, i: (layer, 0, j + n_off)),
        ],
        out_specs=pl.BlockSpec((bm, bn), lambda j, i: (i, j)),
        out_shape=jax.ShapeDtypeStruct((m, n_blocks * bn), out_dtype),
        scratch_shapes=[pltpu.VMEM((k, bn), BF16)],
        compiler_params=_params(2),
        name=name,
    )(a, w)


PROJ_BM = 1024
PROJ_BN = 512
PROJ_TILES = IN_WIDTH // PROJ_BN
PROJ_STEPS = ROWS // PROJ_BM
PROJ_A_TILES = (A_Q_W + 2 * A_KV_W) // PROJ_BN
PROJ_U_TILES = POOL_WIDTH // PROJ_BN
PROJ_Q_TILES = A_Q_W // PROJ_BN
SIDE_ROWS = D_MODEL // PROJ_STEPS
GATE_SIDE_COLS = N_BRANCHES * D_MODEL // PROJ_TILES
BRANCH_SIDE_COLS = D_MODEL // PROJ_TILES
BRANCH_B_ROW = 2048
BRANCH_C_ROW = 3072
BRANCH_ROWS = BRANCH_C_ROW + C_W


def _proj_kernel(h_ref, w_ref, wg_ref, wbr_ref, p_ref, u_ref, wg_o, wbr_o, wb_scr):
    j = pl.program_id(0)

    @pl.when(pl.program_id(1) == 0)
    def _():
        wb_scr[...] = w_ref[...].astype(BF16)

    wg_o[...] = wg_ref[...].astype(BF16)
    wbr_o[...] = wbr_ref[...].astype(BF16)
    acc = jnp.dot(h_ref[...], wb_scr[...], preferred_element_type=F32)
    c0 = PROJ_A_TILES + PROJ_U_TILES
    scale = jnp.where(j < PROJ_Q_TILES, A_Q_SCALE,
                      jnp.where((j >= c0) & (j < c0 + PROJ_Q_TILES), C_Q_SCALE, 1.0))
    p_ref[...] = (acc * scale).astype(BF16)

    @pl.when((j >= PROJ_A_TILES) & (j < c0))
    def _():
        u_ref[...] = acc


def _proj(h, w_in, w_gate, w_branch, layer):
    bm, bn = PROJ_BM, PROJ_BN
    last = PROJ_STEPS - 1
    c0 = PROJ_A_TILES + PROJ_U_TILES
    u_index = lambda j, i: (jnp.where(j < PROJ_A_TILES, 0, jnp.where(j < c0, i, last)),
                            jnp.clip(j - PROJ_A_TILES, 0, PROJ_U_TILES - 1))
    branch_row = lambda i: i + (BRANCH_B_ROW - A_Q_W) // SIDE_ROWS * (i >= A_Q_W // SIDE_ROWS)
    return pl.pallas_call(
        _proj_kernel,
        grid=(PROJ_TILES, PROJ_STEPS),
        in_specs=[
            pl.BlockSpec((bm, D_MODEL), lambda j, i: (i, 0)),
            pl.BlockSpec((None, D_MODEL, bn), lambda j, i: (layer, 0, j)),
            pl.BlockSpec((None, SIDE_ROWS, GATE_SIDE_COLS), lambda j, i: (layer, i, j)),
            pl.BlockSpec((None, SIDE_ROWS, BRANCH_SIDE_COLS), lambda j, i: (layer, i, j)),
        ],
        out_specs=[
            pl.BlockSpec((bm, bn), lambda j, i: (i, j)),
            pl.BlockSpec((bm, bn), u_index),
            pl.BlockSpec((SIDE_ROWS, GATE_SIDE_COLS), lambda j, i: (i, j)),
            pl.BlockSpec((SIDE_ROWS, BRANCH_SIDE_COLS), lambda j, i: (branch_row(i), j)),
        ],
        out_shape=[
            jax.ShapeDtypeStruct((ROWS, IN_WIDTH), BF16),
            jax.ShapeDtypeStruct((ROWS, PROJ_U_TILES * bn), F32),
            jax.ShapeDtypeStruct((D_MODEL, N_BRANCHES * D_MODEL), BF16),
            jax.ShapeDtypeStruct((BRANCH_ROWS, D_MODEL), BF16),
        ],
        scratch_shapes=[pltpu.VMEM((D_MODEL, bn), BF16)],
        compiler_params=_params(2),
        name="proj",
    )(h, w_in, w_gate, w_branch)


A_KEYS = 3 * A_BLOCK
A_NB = SEQ // A_BLOCK
A_GROUP_ROWS = A_GROUP * A_BLOCK
LOG2E = math.log2(math.e)
A_Q_SCALE = HEAD_DIM ** -0.5 * LOG2E


def _attn_a_bias_kernel(o_ref):
    p = pl.program_id(0)
    r = lax.broadcasted_iota(jnp.int32, (A_BLOCK, A_KEYS), 0)
    c = lax.broadcasted_iota(jnp.int32, (A_BLOCK, A_KEYS), 1)
    dist = jnp.abs(r + p * A_BLOCK - c)
    distf = dist.astype(F32)
    slopes = _alibi_slopes(A_HEADS)
    for g in range(A_KV_HEADS):
        for j in range(A_GROUP):
            bias = jnp.where(dist <= A_WINDOW, (-slopes[g * A_GROUP + j] * LOG2E) * distf, NEG_INF)
            o_ref[g, j * A_BLOCK:(j + 1) * A_BLOCK, :] = bias


def _attn_a_bias():
    return pl.pallas_call(
        _attn_a_bias_kernel,
        grid=(3,),
        out_specs=pl.BlockSpec((None, A_KV_HEADS, A_GROUP_ROWS, A_KEYS), lambda p: (p, 0, 0, 0)),
        out_shape=jax.ShapeDtypeStruct((3, A_KV_HEADS, A_GROUP_ROWS, A_KEYS), F32),
        compiler_params=_params(1),
        name="attn_window_bias",
    )()


def _attn_a_kernel(sink_ref, bias_ref, q_ref, k_ref, v_ref, o_ref):
    n = pl.program_id(1)
    start = jnp.clip(n * A_BLOCK - A_BLOCK, 0, SEQ - A_KEYS)
    start = pl.multiple_of(start, A_BLOCK)
    nt = (((1,), (1,)), ((), ()))
    logits = []
    for g in range(A_KV_HEADS):
        k_g = k_ref[pl.ds(start, A_KEYS), g * HEAD_DIM:(g + 1) * HEAD_DIM]
        q_g = jnp.concatenate(
            [q_ref[:, h * HEAD_DIM:(h + 1) * HEAD_DIM]
             for h in range(g * A_GROUP, (g + 1) * A_GROUP)], axis=0)
        logits.append(lax.dot_general(q_g, k_g, nt, preferred_element_type=F32) + bias_ref[g])
    logits = jnp.concatenate(logits, axis=0)
    sink = jnp.concatenate(
        [jnp.full((A_BLOCK, HEAD_DIM), sink_ref[h] * LOG2E, F32) for h in range(A_HEADS)], axis=0)
    m = jnp.maximum(jnp.broadcast_to(jnp.max(logits, axis=-1, keepdims=True), sink.shape), sink)
    p = jnp.exp2(logits - jnp.concatenate([m] * (A_KEYS // HEAD_DIM), axis=1)).astype(BF16)
    sink_term = jnp.exp2(sink - m)
    ones = jnp.ones((A_KEYS, HEAD_DIM), BF16)
    for g in range(A_KV_HEADS):
        rows = slice(g * A_GROUP_ROWS, (g + 1) * A_GROUP_ROWS)
        v_g = v_ref[pl.ds(start, A_KEYS), g * HEAD_DIM:(g + 1) * HEAD_DIM]
        pv = jnp.dot(p[rows], jnp.concatenate([v_g, ones], axis=1), preferred_element_type=F32)
        o = pv[:, :HEAD_DIM] * (1.0 / (pv[:, HEAD_DIM:] + sink_term[rows]))
        for j in range(A_GROUP):
            h = g * A_GROUP + j
            o_ref[:, h * HEAD_DIM:(h + 1) * HEAD_DIM] = o[j * A_BLOCK:(j + 1) * A_BLOCK].astype(BF16)


def _attn_a(proj_a, sink, bias):
    pattern = lambda n: jnp.minimum(n, 1) + jnp.maximum(n - (A_NB - 2), 0)
    return pl.pallas_call(
        _attn_a_kernel,
        grid=(BATCH, A_NB),
        in_specs=[
            pl.BlockSpec(memory_space=pltpu.SMEM),
            pl.BlockSpec((None, A_KV_HEADS, A_GROUP_ROWS, A_KEYS), lambda b, n: (pattern(n), 0, 0, 0)),
            pl.BlockSpec((None, A_BLOCK, A_Q_W), lambda b, n: (b, n, 0)),
            pl.BlockSpec((None, SEQ, A_KV_W), lambda b, n: (b, 0, A_Q_W // A_KV_W)),
            pl.BlockSpec((None, SEQ, A_KV_W), lambda b, n: (b, 0, A_Q_W // A_KV_W + 1)),
        ],
        out_specs=pl.BlockSpec((None, A_BLOCK, A_Q_W), lambda b, n: (b, n, 0)),
        out_shape=jax.ShapeDtypeStruct((BATCH, SEQ, A_Q_W), BF16),
        compiler_params=_params(2),
        name="attn_window",
    )(sink, bias, proj_a, proj_a, proj_a)


def _pool_kernel(u_ref, w_ref, scale_ref, o_ref):
    t = lax.broadcasted_iota(jnp.int32, (SEQ, 1), 0)
    for g, win in enumerate(POOL_WINDOWS):
        r = win // 2
        cols = slice(g * POOL_GROUP_WIDTH, (g + 1) * POOL_GROUP_WIDTH)
        ug = u_ref[:, cols]
        acc = ug
        for d in range(1, r + 1):
            below = pltpu.roll(ug, d, 0)
            above = pltpu.roll(ug, SEQ - d, 0)
            acc = acc + jnp.where(t >= d, below, 0.0) + jnp.where(t < SEQ - d, above, 0.0)
        cnt = (jnp.minimum(t + r + 1, SEQ) - jnp.maximum(t - r, 0)).astype(F32)
        z = (acc / cnt - ug).astype(BF16)
        y = jnp.dot(z, w_ref[g], preferred_element_type=F32) * scale_ref[:, cols]
        o_ref[:, cols] = y.astype(BF16)


def _pool(u, pool_w, pool_scale):
    return pl.pallas_call(
        _pool_kernel,
        grid=(BATCH,),
        in_specs=[
            pl.BlockSpec((None, SEQ, POOL_WIDTH), lambda b: (b, 0, 0)),
            pl.BlockSpec((POOL_GROUPS, POOL_GROUP_WIDTH, POOL_GROUP_WIDTH), lambda b: (0, 0, 0)),
            pl.BlockSpec((1, POOL_WIDTH), lambda b: (0, 0)),
        ],
        out_specs=pl.BlockSpec((None, SEQ, POOL_WIDTH), lambda b: (b, 0, 0)),
        out_shape=jax.ShapeDtypeStruct((BATCH, SEQ, POOL_WIDTH), BF16),
        compiler_params=_params(1),
        name="pool",
    )(u, pool_w, pool_scale.reshape(1, POOL_WIDTH))


C_BQ = 512
C_NQ = SEQ // C_BQ
C_BIAS_BLOCKS = 2 * C_NQ - 1
C_Q_SCALE = C_HALF_DIM ** -0.5 * LOG2E


def _attn_c_bias_kernel(slopes_ref, o_ref):
    s = slopes_ref[pl.program_id(0)] * LOG2E
    r = lax.broadcasted_iota(jnp.int32, (C_BQ, C_BQ), 0)
    cc = lax.broadcasted_iota(jnp.int32, (C_BQ, C_BQ), 1)
    for cb in range(C_BIAS_BLOCKS):
        dist = jnp.abs(r - cc - (cb - (C_NQ - 1)) * C_BQ)
        o_ref[cb] = s * dist.astype(F32)


def _attn_c_bias():
    slopes = jnp.asarray(_alibi_slopes(C_HEADS), F32)
    return pl.pallas_call(
        _attn_c_bias_kernel,
        grid=(C_HEADS,),
        in_specs=[pl.BlockSpec(memory_space=pltpu.SMEM)],
        out_specs=pl.BlockSpec((None, C_BIAS_BLOCKS, C_BQ, C_BQ), lambda h: (h, 0, 0, 0)),
        out_shape=jax.ShapeDtypeStruct((C_HEADS, C_BIAS_BLOCKS, C_BQ, C_BQ), F32),
        compiler_params=_params(1),
        name="attn_diff_bias",
    )(slopes)


C_BLOCKS = BATCH * C_HEADS * C_NQ
C_STAGES = 3


def _attn_c_block(n):
    n = jnp.clip(n, 0, C_BLOCKS - 1)
    return n // (C_HEADS * C_NQ), (n // C_NQ) % C_HEADS, n % C_NQ


def _attn_c_step(lam_init, lam_ref, g_ref, bias_ref, q_ref, v_ref, o_ref, kt_ref,
                 qi, logit_w, max_w, logit_r, max_r, prob_w, sum_w, prob_r, sum_r):
    lv = lam_ref[...]
    lam = (jnp.exp(jnp.sum(lv[0:1] * lv[1:2], axis=-1, keepdims=True))
           - jnp.exp(jnp.sum(lv[2:3] * lv[3:4], axis=-1, keepdims=True)) + lam_init)
    v = v_ref[...]
    o1 = jnp.dot(prob_r[0], v, preferred_element_type=F32)
    o2 = jnp.dot(prob_r[1], v, preferred_element_type=F32)
    o = o1 * (1.0 / sum_r[0]) - o2 * (lam / sum_r[1])
    o = _rms(o, g_ref[...], SUBLN_EPS) * (1.0 - lam_init)
    o_ref[...] = o.astype(BF16)

    for half in range(2):
        p = jnp.exp2(logit_r[half] - max_r[half])
        sum_w[half] = jnp.sum(p, axis=-1, keepdims=True)
        prob_w[half] = p.astype(BF16)

    q = q_ref[...]
    lane = lax.broadcasted_iota(jnp.int32, (C_BQ, HEAD_DIM), 1)
    zero = jnp.zeros_like(q)
    kt = kt_ref[...]
    bias = jnp.concatenate(
        [bias_ref[C_NQ - 1 - qi + kb] for kb in range(C_NQ)], axis=1)
    for half, keep in enumerate((lane < C_HALF_DIM, lane >= C_HALF_DIM)):
        logits = jnp.dot(jnp.where(keep, q, zero), kt, preferred_element_type=F32) - bias
        logit_w[half] = logits
        max_w[half] = jnp.max(logits, axis=-1, keepdims=True)


def _attn_c_kernel(lam_init, lam_ref, g_ref, bias_ref, q_ref, k_ref, v_ref, o_ref, kt_ref,
                   logit0, logit1, max0, max1, prob0, prob1, sum0, sum1):
    t = pl.program_id(0)
    qi = jnp.minimum(t, C_BLOCKS - 1) % C_NQ

    @pl.when(t == 0)
    def _():
        logit1[...] = jnp.zeros_like(logit1)
        max1[...] = jnp.zeros_like(max1)
        prob0[...] = jnp.zeros_like(prob0)
        sum0[...] = jnp.ones_like(sum0)

    @pl.when((qi == 0) & (t < C_BLOCKS))
    def _():
        kt_ref[...] = k_ref[...].astype(F32).T.astype(BF16)

    step = functools.partial(_attn_c_step, lam_init, lam_ref, g_ref, bias_ref, q_ref, v_ref,
                             o_ref, kt_ref, qi)

    @pl.when(t % 2 == 0)
    def _():
        step(logit0, max0, logit1, max1, prob1, sum1, prob0, sum0)

    @pl.when(t % 2 == 1)
    def _():
        step(logit1, max1, logit0, max0, prob0, sum0, prob1, sum1)


def _attn_c(proj_c, bias, lam_vecs, subln_g, lam_init):
    col0 = (IN_WIDTH - 3 * C_W) // HEAD_DIM

    def q_index(t):
        b, h, i = _attn_c_block(t)
        return (b, i, col0 + h)

    def k_index(t):
        b, h, _ = _attn_c_block(t)
        return (b, 0, col0 + C_HEADS + h)

    def v_index(t):
        b, h, _ = _attn_c_block(t - (C_STAGES - 1))
        return (b, 0, col0 + 2 * C_HEADS + h)

    def o_index(t):
        b, h, i = _attn_c_block(t - (C_STAGES - 1))
        return (b, i, h)

    half_rows = lambda width, dtype: pltpu.VMEM((2, C_BQ, width), dtype)
    return pl.pallas_call(
        functools.partial(_attn_c_kernel, lam_init),
        grid=(C_BLOCKS + C_STAGES - 1,),
        in_specs=[
            pl.BlockSpec((4, C_HALF_DIM), lambda t: (0, 0)),
            pl.BlockSpec((1, HEAD_DIM), lambda t: (0, 0)),
            pl.BlockSpec((None, C_BIAS_BLOCKS, C_BQ, C_BQ), lambda t: (_attn_c_block(t)[1], 0, 0, 0)),
            pl.BlockSpec((None, C_BQ, HEAD_DIM), q_index),
            pl.BlockSpec((None, SEQ, HEAD_DIM), k_index),
            pl.BlockSpec((None, SEQ, HEAD_DIM), v_index),
        ],
        out_specs=pl.BlockSpec((None, C_BQ, HEAD_DIM), o_index),
        out_shape=jax.ShapeDtypeStruct((BATCH, SEQ, C_W), BF16),
        scratch_shapes=[
            pltpu.VMEM((HEAD_DIM, SEQ), BF16),
            half_rows(SEQ, F32), half_rows(SEQ, F32),
            half_rows(1, F32), half_rows(1, F32),
            half_rows(SEQ, BF16), half_rows(SEQ, BF16),
            half_rows(1, F32), half_rows(1, F32),
        ],
        compiler_params=_params(1),
        name="attn_diff",
    )(lam_vecs, subln_g.reshape(1, HEAD_DIM), bias, proj_c, proj_c, proj_c)


MERGE_BM = 1024
MERGE_BN = 256


def _merge_kernel(h_ref, oa_ref, ob_ref, oc_ref, wg0_ref, wg1_ref, wg2_ref,
                  b0_ref, b1_ref, b2_ref, wa_ref, wb_ref, wc_ref, wo_ref, o_ref, wo_o):
    wo_o[...] = wo_ref[...].astype(BF16)
    h = h_ref[...]

    def branch(wg_ref, b_ref, x_ref, w_ref):
        gate = jax.nn.sigmoid(jnp.dot(h, wg_ref[...], preferred_element_type=F32) + b_ref[...])
        return gate * jnp.dot(x_ref[...], w_ref[...], preferred_element_type=F32)

    merged = (branch(wg0_ref, b0_ref, oa_ref, wa_ref)
              + branch(wg1_ref, b1_ref, ob_ref, wb_ref)
              + branch(wg2_ref, b2_ref, oc_ref, wc_ref))
    o_ref[...] = merged.astype(BF16)


def _merge(h, oa, ob, oc, w_gate, b_gate, w_branch, w_out, layer):
    bm, bn = MERGE_BM, MERGE_BN
    nb = D_MODEL // bn
    wo_rows = D_MODEL // (ROWS // bm)
    row = lambda width: pl.BlockSpec((bm, width), lambda i, j: (i, 0))
    gate_w = lambda br: pl.BlockSpec((D_MODEL, bn), lambda i, j: (0, j + br * nb))
    gate_b = lambda br: pl.BlockSpec((1, bn), lambda i, j: (0, j + br * nb))
    branch_w = lambda width, row0: pl.BlockSpec((width, bn), lambda i, j: (row0 // width, j))
    b_gate = b_gate.reshape(1, N_BRANCHES * D_MODEL)
    return pl.pallas_call(
        _merge_kernel,
        grid=(ROWS // bm, nb),
        in_specs=[
            row(D_MODEL), row(A_Q_W), row(POOL_WIDTH), row(C_W),
            gate_w(0), gate_w(1), gate_w(2),
            gate_b(0), gate_b(1), gate_b(2),
            branch_w(A_Q_W, 0), branch_w(POOL_WIDTH, BRANCH_B_ROW), branch_w(C_W, BRANCH_C_ROW),
            pl.BlockSpec((None, wo_rows, bn), lambda i, j: (layer, i, j)),
        ],
        out_specs=[pl.BlockSpec((bm, bn), lambda i, j: (i, j)),
                   pl.BlockSpec((wo_rows, bn), lambda i, j: (i, j))],
        out_shape=[jax.ShapeDtypeStruct((ROWS, D_MODEL), BF16),
                   jax.ShapeDtypeStruct((D_MODEL, D_MODEL), BF16)],
        compiler_params=_params(2),
        name="gated_merge",
    )(h, oa, ob, oc, w_gate, w_gate, w_gate, b_gate, b_gate, b_gate,
      w_branch, w_branch, w_branch, w_out)


FFN_BM = 2048
FFN_BF = 256


def _ffn_up_kernel(h_ref, wg_ref, wu_ref, wd_ref, o_ref, wd_o, wgb_ref, wub_ref):
    @pl.when(pl.program_id(1) == 0)
    def _():
        wgb_ref[...] = wg_ref[...].astype(BF16)
        wub_ref[...] = wu_ref[...].astype(BF16)

    wd_o[...] = wd_ref[...].astype(BF16)
    h = h_ref[...]
    a = jnp.dot(h, wgb_ref[...], preferred_element_type=F32)
    b = jnp.dot(h, wub_ref[...], preferred_element_type=F32)
    o_ref[...] = (a * jax.nn.sigmoid(a) * b).astype(BF16)


def _ffn_up(h, wg, wu, wd, layer):
    bm, bf = FFN_BM, FFN_BF
    steps = ROWS // bm
    wd_cols = D_MODEL // steps
    w_spec = pl.BlockSpec((None, D_MODEL, bf), lambda j, i: (layer, 0, j))
    return pl.pallas_call(
        _ffn_up_kernel,
        grid=(FFN_HIDDEN // bf, steps),
        in_specs=[pl.BlockSpec((bm, D_MODEL), lambda j, i: (i, 0)), w_spec, w_spec,
                  pl.BlockSpec((None, bf, wd_cols), lambda j, i: (layer, j, i))],
        out_specs=[pl.BlockSpec((bm, bf), lambda j, i: (i, j)),
                   pl.BlockSpec((bf, wd_cols), lambda j, i: (j, i))],
        out_shape=[jax.ShapeDtypeStruct((ROWS, FFN_HIDDEN), BF16),
                   jax.ShapeDtypeStruct((FFN_HIDDEN, D_MODEL), BF16)],
        scratch_shapes=[pltpu.VMEM((D_MODEL, bf), BF16), pltpu.VMEM((D_MODEL, bf), BF16)],
        compiler_params=_params(2),
        name="ffn_up",
    )(h, wg, wu, wd)


def kernel(x, c, ada_w, ada_b, mix_pre_g, mix_post_g, w_in, attn_sink, pool_w, pool_scale, diff_lambda, diff_subln_g, w_gate, b_gate, w_branch, w_out, ffn_pre_g, ffn_post_g, ffn_w_gate, ffn_w_up, ffn_w_down):
    c_pad = jnp.pad(c, ((0, ADA_ROWS - BATCH), (0, 0)))
    mod = _ada_mod(c_pad, ada_w, ada_b)[:, :BATCH]
    mod = mod.reshape(DEPTH * BATCH * ADA_CHUNKS, 1, D_MODEL)

    x = x.reshape(ROWS, D_MODEL)
    h = _norm_mod(x, mix_pre_g[0], mod, 0, 0)
    bias_a = _attn_a_bias()
    bias_c = _attn_c_bias()
    for l in range(DEPTH):
        lam_init = 0.8 - 0.6 * math.exp(-0.3 * l)
        proj, proj_u, w_gate_b, w_branch_b = _proj(h, w_in, w_gate, w_branch, l)
        proj = proj.reshape(BATCH, SEQ, IN_WIDTH)

        oa = _attn_a(proj, attn_sink[l], bias_a)
        ob = _pool(proj_u.reshape(BATCH, SEQ, POOL_WIDTH), pool_w[l].astype(BF16), pool_scale[l])
        oc = _attn_c(proj, bias_c, diff_lambda[l], diff_subln_g[l], lam_init)

        merged, w_out_b = _merge(
            h, oa.reshape(ROWS, A_Q_W), ob.reshape(ROWS, POOL_WIDTH), oc.reshape(ROWS, C_W),
            w_gate_b, b_gate[l], w_branch_b, w_out, l)
        x, h = _out_resid(merged, w_out_b, x, mix_post_g[l], mod, l, 2, ffn_pre_g[l], l, 3)

        hid, w_down_b = _ffn_up(h, ffn_w_gate, ffn_w_up, ffn_w_down, l)
        y = _matmul(hid, w_down_b, F32, 512, 512, name="ffn_down")
        if l + 1 < DEPTH:
            x, h = _resid(x, y, ffn_post_g[l], mod, l, 5, nxt=(mix_pre_g[l + 1], l + 1, 0))
        else:
            x = _resid(x, y, ffn_post_g[l], mod, l, 5)
    return x.reshape(BATCH, SEQ, D_MODEL)
```
